```python
import math
import jax
import jax.numpy as jnp
from jax import lax
import numpy as np

D_MODEL = 1024
BATCH = 8
SEQ = 4096
DEPTH = 1

CTX_LEN = 256
GRID_W = 64

A_HEADS = 4
A_DK = 128
A_DV = 128
A_CHUNK = 64
CONV_K = 5
B_HEADS = 4
B_DK = 128
B_DV = 128
B_CHUNK = 16
N_BRANCH = 2

A_QK = A_HEADS * A_DK
A_V = A_HEADS * A_DV
B_QK = B_HEADS * B_DK
B_V = B_HEADS * B_DV
CONV_DIM = 2 * A_QK + A_V
IN_SPLITS = (A_QK, A_QK, A_V, 2 * A_HEADS, 2 * A_HEADS, A_V,
             B_QK, B_QK, B_QK, B_V, B_V, N_BRANCH * D_MODEL)
D_IN = 2 * A_QK + 2 * A_V + 4 * A_HEADS + 3 * B_QK + 2 * B_V + N_BRANCH * D_MODEL

DEEPNORM_ALPHA = (2 * DEPTH) ** 0.25
DEEPNORM_BETA = (8 * DEPTH) ** -0.25
LN_EPS = 1e-6
RMS_EPS = 1e-6
L2_EPS = 1e-6

kernel_name = 'hybrid_gdn_hgrn2_dit_block'


def layer_norm(t):
    t = t.astype(jnp.float32)
    mu = jnp.mean(t, axis=-1, keepdims=True)
    var = jnp.mean(jnp.square(t - mu), axis=-1, keepdims=True)
    return (t - mu) * lax.rsqrt(var + LN_EPS)


def l2norm(t):
    return t * lax.rsqrt(jnp.sum(t * t, axis=-1, keepdims=True) + L2_EPS)


def rms_norm_gated(o, gain, gate):
    o = o * lax.rsqrt(jnp.mean(o * o, axis=-1, keepdims=True) + RMS_EPS) * gain
    return o.reshape(*o.shape[:-2], -1) * jax.nn.silu(gate)


def split_cols(z):
    out, start = [], 0
    for n in IN_SPLITS:
        out.append(z[..., start:start + n])
        start += n
    return out


def centred_dwconv(u, w):
    u = u.astype(jnp.float32)
    return lax.conv_general_dilated(
        u, w.astype(jnp.float32)[:, None, :], window_strides=(1,),
        padding=((CONV_K // 2, CONV_K // 2),),
        dimension_numbers=('NWC', 'WIO', 'NWC'), feature_group_count=u.shape[-1])


def to_colmajor(t, rows):
    bsz, rest = t.shape[0], t.shape[2:]
    return t.reshape(bsz, rows, GRID_W, *rest).swapaxes(1, 2).reshape(bsz, rows * GRID_W, *rest)


def from_colmajor(t, rows):
    bsz, rest = t.shape[0], t.shape[2:]
    return t.reshape(bsz, GRID_W, rows, *rest).swapaxes(1, 2).reshape(bsz, rows * GRID_W, *rest)


def _to_chunks(t, chunk):
    bsz, length, heads = t.shape[:3]
    t = t.reshape(bsz, length // chunk, chunk, heads, *t.shape[3:])
    return jnp.moveaxis(t, 3, 1)


def _from_chunks(o):
    n, bsz, heads, chunk, dv = o.shape
    return o.transpose(1, 0, 3, 2, 4).reshape(bsz, n * chunk, heads, dv)


def gdn_chunk_scan(q, k, v, g, beta, s0):
    dk = q.shape[-1]
    q, k, v = (_to_chunks(t, A_CHUNK) for t in (q, k, v))
    g, beta = _to_chunks(g, A_CHUNK), _to_chunks(beta, A_CHUNK)
    G = jnp.cumsum(g, axis=-1)
    idx = jnp.arange(A_CHUNK)
    incl = idx[:, None] >= idx[None, :]
    strict = idx[:, None] > idx[None, :]
    diff = G[..., :, None] - G[..., None, :]
    decay = jnp.where(incl, jnp.exp(jnp.where(incl, diff, 0.0)), 0.0)
    kb = k * beta[..., None]
    a_mat = jnp.where(strict, jnp.einsum('bhncd,bhnsd->bhncs', kb, k) * decay, 0.0)
    eye = jnp.eye(A_CHUNK, dtype=a_mat.dtype)
    rhs = jnp.concatenate([kb * jnp.exp(G)[..., None], v * beta[..., None]], axis=-1)
    sol = lax.linalg.triangular_solve(a_mat + eye, rhs, left_side=True, lower=True,
                                      unit_diagonal=True)
    w, u = sol[..., :dk], sol[..., dk:]
    attn = jnp.einsum('bhncd,bhnsd->bhncs', q, k) * decay
    qg = q * jnp.exp(G)[..., None]
    g_last = G[..., -1:]
    kd = k * jnp.exp(g_last - G)[..., None]
    xs = tuple(jnp.moveaxis(t, 2, 0) for t in (w, u, qg, attn, kd, jnp.exp(g_last[..., 0])))

    def step(S, inp):
        w_c, u_c, qg_c, attn_c, kd_c, gl_c = inp
        v_new = u_c - jnp.einsum('bhcd,bhdv->bhcv', w_c, S)
        o_c = jnp.einsum('bhcd,bhdv->bhcv', qg_c, S) + jnp.einsum('bhcs,bhsv->bhcv', attn_c, v_new)
        S = S * gl_c[..., None, None] + jnp.einsum('bhcd,bhcv->bhdv', kd_c, v_new)
        return S, o_c

    s_final, o = lax.scan(step, s0, xs)
    return _from_chunks(o), s_final


def gla_chunk_scan(q, k, v, g, s0):
    q, k, v, g = (_to_chunks(t, B_CHUNK) for t in (q, k, v, g))
    G = jnp.cumsum(g, axis=-2)
    idx = jnp.arange(B_CHUNK)
    incl = idx[:, None] >= idx[None, :]
    qg = q * jnp.exp(G)
    kg = k * jnp.exp(-G)
    attn = jnp.where(incl, jnp.einsum('bhncd,bhnsd->bhncs', qg, kg), 0.0)
    g_last = G[..., -1:, :]
    kd = k * jnp.exp(g_last - G)
    xs = tuple(jnp.moveaxis(t, 2, 0) for t in (qg, attn, v, kd, jnp.exp(g_last[..., 0, :])))

    def step(S, inp):
        qg_c, attn_c, v_c, kd_c, gl_c = inp
        o_c = jnp.einsum('bhcd,bhdv->bhcv', qg_c, S) + jnp.einsum('bhcs,bhsv->bhcv', attn_c, v_c)
        S = S * gl_c[..., :, None] + jnp.einsum('bhcd,bhcv->bhdv', kd_c, v_c)
        return S, o_c

    s_final, o = lax.scan(step, s0, xs)
    return _from_chunks(o), s_final


def prefix_scan(scan_fn, ctx_args, lat_args, s0, reverse):
    flip = (lambda t: jnp.flip(t, axis=1)) if reverse else (lambda t: t)
    o_ctx, s_ctx = scan_fn(*[flip(t) for t in ctx_args], s0)
    o_lat, _ = scan_fn(*[flip(t) for t in lat_args], s_ctx)
    return flip(o_ctx), flip(o_lat)


def project_stream(u, w_in_l, conv_w_l, a_log_l, dt_bias_l, lb_l):
    bsz, length, _ = u.shape
    (a_q, a_k, a_v, a_alpha, a_beta, a_gate,
     b_q, b_f_fwd, b_f_bwd, b_i, b_gate, merge) = split_cols(u @ w_in_l)
    qkv = jax.nn.silu(centred_dwconv(jnp.concatenate([a_q, a_k, a_v], axis=-1), conv_w_l))
    a_q = l2norm(qkv[..., :A_QK].reshape(bsz, length, A_HEADS, A_DK)) * A_DK ** -0.5
    a_k = l2norm(qkv[..., A_QK:2 * A_QK].reshape(bsz, length, A_HEADS, A_DK))
    a_v = qkv[..., 2 * A_QK:].reshape(bsz, length, A_HEADS, A_DV)
    a_g = -jnp.exp(a_log_l) * jax.nn.softplus(a_alpha.reshape(bsz, length, 2, A_HEADS) + dt_bias_l)
    a_b = jax.nn.sigmoid(a_beta.reshape(bsz, length, 2, A_HEADS))
    f_logit = jnp.stack([b_f_fwd, b_f_bwd], axis=2)
    f = lb_l + (1.0 - lb_l) * jax.nn.sigmoid(f_logit)
    b_g = jnp.log(f).reshape(bsz, length, 2, B_HEADS, B_DK)
    b_k = ((1.0 - lb_l) * jax.nn.sigmoid(-f_logit)).reshape(bsz, length, 2, B_HEADS, B_DK)
    b_q = jax.nn.silu(b_q).reshape(bsz, length, B_HEADS, B_DK) * B_DK ** -0.5
    b_i = b_i.reshape(bsz, length, B_HEADS, B_DV)
    return {'a_q': a_q, 'a_k': a_k, 'a_v': a_v, 'a_g': a_g, 'a_b': a_b, 'a_gate': a_gate,
            'b_q': b_q, 'b_k': b_k, 'b_g': b_g, 'b_i': b_i, 'b_gate': b_gate, 'merge': merge}


def merge_branches(o_a, o_b, p, a_norm_l, b_norm_l, w_a_out_l, w_b_out_l, w_out_l):
    y_a = rms_norm_gated(o_a, a_norm_l, p['a_gate']) @ w_a_out_l
    y_b = rms_norm_gated(o_b, b_norm_l, p['b_gate']) @ w_b_out_l
    gates = jax.nn.sigmoid(p['merge'].reshape(*p['merge'].shape[:-1], N_BRANCH, D_MODEL))
    return (gates[..., 0, :] * y_a + gates[..., 1, :] * y_b) @ w_out_l


def setup_inputs(seed: int = 0) -> dict:
    key = jax.random.key(seed)
    ks = jax.random.split(key, 20)
    f32 = jnp.float32

    def nrm(k, shape, scale):
        return jax.random.normal(k, shape, f32) * scale

    x = nrm(ks[0], (BATCH, SEQ, D_MODEL), 1.0)
    c = nrm(ks[1], (BATCH, D_MODEL), 1.0)
    ctx = nrm(ks[2], (BATCH, CTX_LEN, D_MODEL), 1.0)
    c_ctx = nrm(ks[3], (D_MODEL,), 1.0)
    w_mod = nrm(ks[4], (DEPTH, D_MODEL, 3 * D_MODEL), D_MODEL ** -0.5)
    b_mod = nrm(ks[5], (DEPTH, 3 * D_MODEL), 0.02)
    w_in = nrm(ks[6], (DEPTH, D_MODEL, D_IN), D_MODEL ** -0.5)
    conv_w = nrm(ks[7], (DEPTH, CONV_K, CONV_DIM), CONV_K ** -0.5)
    a_log = jnp.log(jax.random.uniform(ks[8], (DEPTH, 2, A_HEADS), f32, 1.0, 16.0))
    dt = jnp.exp(jax.random.uniform(ks[9], (DEPTH, 2, A_HEADS), f32, math.log(1e-3), math.log(1e-1)))
    dt_bias = dt + jnp.log(-jnp.expm1(-dt))
    lb_param = nrm(ks[10], (DEPTH + 1, 2, B_QK), 0.1)
    a_norm_g = 1.0 + nrm(ks[11], (DEPTH, A_DV), 0.02)
    b_norm_g = 1.0 + nrm(ks[12], (DEPTH, B_DV), 0.02)
    w_a_out = nrm(ks[13], (DEPTH, A_V, D_MODEL), DEEPNORM_BETA * A_V ** -0.5)
    w_b_out = nrm(ks[14], (DEPTH, B_V, D_MODEL), DEEPNORM_BETA * B_V ** -0.5)
    w_out = nrm(ks[15], (DEPTH, D_MODEL, D_MODEL), DEEPNORM_BETA * D_MODEL ** -0.5)
    ln_g = 1.0 + nrm(ks[16], (DEPTH, D_MODEL), 0.02)
    ln_b = nrm(ks[17], (DEPTH, D_MODEL), 0.02)
    return {'x': x, 'c': c, 'ctx': ctx, 'c_ctx': c_ctx, 'w_mod': w_mod, 'b_mod': b_mod,
            'w_in': w_in, 'conv_w': conv_w, 'a_log': a_log, 'dt_bias': dt_bias,
            'lb_param': lb_param, 'a_norm_g': a_norm_g, 'b_norm_g': b_norm_g,
            'w_a_out': w_a_out, 'w_b_out': w_b_out, 'w_out': w_out, 'ln_g': ln_g, 'ln_b': ln_b}


def reference(x, c, ctx, c_ctx, w_mod, b_mod, w_in, conv_w, a_log, dt_bias, lb_param,
              a_norm_g, b_norm_g, w_a_out, w_b_out, w_out, ln_g, ln_b):
    f32 = jnp.float32
    bsz = x.shape[0]
    rows = x.shape[1] // GRID_W
    lb_all = jnp.cumsum(jax.nn.softmax(lb_param.astype(f32), axis=0), axis=0)
    h_lat = x.astype(f32)
    h_ctx = ctx.astype(f32)
    s0_a = jnp.zeros((bsz, A_HEADS, A_DK, A_DV), f32)
    s0_b = jnp.zeros((bsz, B_HEADS, B_DK, B_DV), f32)
    cm = lambda t: to_colmajor(t, rows)
    for l in range(DEPTH):
        shift_l, scale_l, gate_l = jnp.split((jax.nn.silu(c) @ w_mod[l] + b_mod[l])[:, None, :], 3, axis=-1)
        shift_c, scale_c, gate_c = jnp.split(jax.nn.silu(c_ctx) @ w_mod[l] + b_mod[l], 3, axis=-1)
        u_lat = layer_norm(h_lat) * (1.0 + scale_l) + shift_l
        u_ctx = layer_norm(h_ctx) * (1.0 + scale_c) + shift_c
        pl = project_stream(u_lat, w_in[l], conv_w[l], a_log[l], dt_bias[l], lb_all[l])
        pc = project_stream(u_ctx, w_in[l], conv_w[l], a_log[l], dt_bias[l], lb_all[l])

        oa_ctx, oa_lat = 0.0, 0.0
        for d in range(2):
            o_c, o_l = prefix_scan(
                gdn_chunk_scan,
                (pc['a_q'], pc['a_k'], pc['a_v'], pc['a_g'][:, :, d], pc['a_b'][:, :, d]),
                (pl['a_q'], pl['a_k'], pl['a_v'], pl['a_g'][:, :, d], pl['a_b'][:, :, d]),
                s0_a, reverse=(d == 1))
            oa_ctx = oa_ctx + o_c
            oa_lat = oa_lat + o_l

        ob_ctx, ob_lat_cm = 0.0, 0.0
        for d in range(2):
            o_c, o_l = prefix_scan(
                gla_chunk_scan,
                (pc['b_q'], pc['b_k'][:, :, d], pc['b_i'], pc['b_g'][:, :, d]),
                (cm(pl['b_q']), cm(pl['b_k'][:, :, d]), cm(pl['b_i']), cm(pl['b_g'][:, :, d])),
                s0_b, reverse=(d == 1))
            ob_ctx = ob_ctx + o_c
            ob_lat_cm = ob_lat_cm + o_l
        ob_lat = from_colmajor(ob_lat_cm, rows)

        sub_lat = merge_branches(oa_lat, ob_lat, pl, a_norm_g[l], b_norm_g[l],
                                 w_a_out[l], w_b_out[l], w_out[l])
        if l < DEPTH - 1:
            sub_ctx = merge_branches(oa_ctx, ob_ctx, pc, a_norm_g[l], b_norm_g[l],
                                     w_a_out[l], w_b_out[l], w_out[l])
            h_ctx = layer_norm(DEEPNORM_ALPHA * h_ctx + gate_c * sub_ctx) * ln_g[l] + ln_b[l]
        h_lat = layer_norm(DEEPNORM_ALPHA * h_lat + gate_l * sub_lat) * ln_g[l] + ln_b[l]
    return h_lat.astype(x.dtype)
```

```python
import functools
import math

import jax
import jax.numpy as jnp
from jax import lax
from jax.experimental import pallas as pl
from jax.experimental.pallas import tpu as pltpu

D_MODEL = 1024
GRID_W = 64
HEADS = 4
DH = 128
QK = HEADS * DH
CONV_K = 5
CHUNK = 64
SUB = 32
DEPTH = 1
DEEPNORM_ALPHA = (2 * DEPTH) ** 0.25
LN_EPS = 1e-6
RMS_EPS = 1e-6
L2_EPS = 1e-6

TOK_TILE = 512
COLS_PER_TILE = TOK_TILE // GRID_W
N_PRE = 3 * QK + 128 + 4 * QK
N_POST = 2 * QK + 2 * D_MODEL
VMEM_LIMIT = 56 * 1024 * 1024

F32 = jnp.float32
BF16 = jnp.bfloat16


def _dot(a, b):
    return jnp.dot(a.astype(BF16), b.astype(BF16), preferred_element_type=F32)


def _dot_nt(a, b):
    return lax.dot_general(a.astype(BF16), b.astype(BF16), (((1,), (1,)), ((), ())),
                           preferred_element_type=F32)


def _dot_exact(a, b):
    return jnp.dot(a, b, preferred_element_type=F32, precision=lax.Precision.HIGHEST)


def _sigmoid_pair(x):
    t = jnp.exp(-jnp.abs(x))
    r = 1.0 / (1.0 + t)
    tr = t * r
    pos = x >= 0
    return jnp.where(pos, r, tr), jnp.where(pos, tr, r)


def _silu(x):
    return x * _sigmoid_pair(x)[0]


def _layer_norm(t):
    mu = jnp.mean(t, axis=-1, keepdims=True)
    tc = t - mu
    var = jnp.mean(tc * tc, axis=-1, keepdims=True)
    return tc * lax.rsqrt(var + LN_EPS)


def _mod_kernel(c_ref, w_ref, b_ref, o_ref):
    o_ref[...] = _dot_exact(_silu(c_ref[...]), w_ref[...]) + b_ref[...]


def _modulation(c, c_ctx, w_mod, b_mod):
    bsz = c.shape[0]
    rows = 16
    cc = jnp.zeros((rows, D_MODEL), F32).at[:bsz].set(c).at[bsz].set(c_ctx)
    nblk = 3
    out = pl.pallas_call(
        _mod_kernel,
        grid=(nblk,),
        in_specs=[pl.BlockSpec((rows, D_MODEL), lambda j: (0, 0)),
                  pl.BlockSpec((D_MODEL, D_MODEL), lambda j: (0, j)),
                  pl.BlockSpec((1, D_MODEL), lambda j: (0, j))],
        out_specs=pl.BlockSpec((rows, D_MODEL), lambda j: (0, j)),
        out_shape=jax.ShapeDtypeStruct((rows, 3 * D_MODEL), F32),
    )(cc, w_mod, b_mod.reshape(1, 3 * D_MODEL))
    return out


def _load_tile(ref, width, cm):
    if cm:
        return jnp.concatenate([ref[0, :, cl * width:(cl + 1) * width] for cl in range(COLS_PER_TILE)], axis=0)
    return ref[...]


def _store_raster(ref, val, width, cm):
    if cm:
        for cl in range(COLS_PER_TILE):
            ref[0, :, cl * width:(cl + 1) * width] = val[cl * GRID_W:(cl + 1) * GRID_W, :].astype(ref.dtype)
    else:
        ref[...] = val.astype(ref.dtype)


def _store_cm(ref, val, cm):
    if cm:
        for cl in range(COLS_PER_TILE):
            ref[0, cl] = val[cl * GRID_W:(cl + 1) * GRID_W, :].astype(ref.dtype)
    else:
        ref[...] = val.astype(ref.dtype)


def _pre_kernel(x_ref, shift_ref, scale_ref, w_ref, alog_ref, dtb_ref, lbp_ref,
                aqkv_ref, ab_ref, bq_ref, bg_ref, bk_ref, bi_ref, *, cm):
    xt = _load_tile(x_ref, D_MODEL, cm)
    u = (_layer_norm(xt) * (1.0 + scale_ref[0]) + shift_ref[0]).astype(BF16)

    for g in range(3):
        z = jnp.dot(u, w_ref[:, g * QK:(g + 1) * QK], preferred_element_type=F32)
        if cm:
            for cl in range(COLS_PER_TILE):
                aqkv_ref[0, :, cl * 3 * QK + g * QK: cl * 3 * QK + (g + 1) * QK] = (
                    z[cl * GRID_W:(cl + 1) * GRID_W, :].astype(aqkv_ref.dtype))
        else:
            aqkv_ref[:, g * QK:(g + 1) * QK] = z.astype(aqkv_ref.dtype)

    off = 3 * QK
    z = jnp.dot(u, w_ref[:, off:off + 128], preferred_element_type=F32)
    zs = z + dtb_ref[...]
    softplus = jnp.maximum(zs, 0.0) + jnp.log(1.0 + jnp.exp(-jnp.abs(zs)))
    a_g = -jnp.exp(alog_ref[...]) * softplus
    a_b = _sigmoid_pair(z)[0]
    lane = lax.broadcasted_iota(jnp.int32, z.shape, 1)
    _store_raster(ab_ref, jnp.where(lane < 2 * HEADS, a_g, a_b), 128, cm)

    off += 128
    z = jnp.dot(u, w_ref[:, off:off + QK], preferred_element_type=F32)
    _store_cm(bq_ref, _silu(z) * DH ** -0.5, cm)

    off += QK
    p0 = lbp_ref[0:1, :]
    p1 = lbp_ref[1:2, :]
    pm = jnp.maximum(p0, p1)
    e0 = jnp.exp(p0 - pm)
    lb = e0 / (e0 + jnp.exp(p1 - pm))
    for d in range(2):
        z = jnp.dot(u, w_ref[:, off + d * QK: off + (d + 1) * QK], preferred_element_type=F32)
        s_pos, s_neg = _sigmoid_pair(z)
        lbd = lb[:, d * QK:(d + 1) * QK]
        f = lbd + (1.0 - lbd) * s_pos
        gl = jnp.log(f)
        kk = (1.0 - lbd) * s_neg
        if cm:
            for cl in range(COLS_PER_TILE):
                bg_ref[0, cl, :, d * QK:(d + 1) * QK] = gl[cl * GRID_W:(cl + 1) * GRID_W, :]
                bk_ref[0, cl, :, d * QK:(d + 1) * QK] = kk[cl * GRID_W:(cl + 1) * GRID_W, :].astype(bk_ref.dtype)
        else:
            bg_ref[:, d * QK:(d + 1) * QK] = gl
            bk_ref[:, d * QK:(d + 1) * QK] = kk.astype(bk_ref.dtype)

    off += 2 * QK
    z = jnp.dot(u, w_ref[:, off:off + QK], preferred_element_type=F32)
    _store_cm(bi_ref, z, cm)


def _pre_project(tokens, shift, scale, w_pre, alog_vec, dtb_vec, lbp, *, cm):
    consts = [pl.BlockSpec((D_MODEL, N_PRE), lambda *_: (0, 0)),
              pl.BlockSpec((1, 128), lambda *_: (0, 0)),
              pl.BlockSpec((1, 128), lambda *_: (0, 0)),
              pl.BlockSpec((2, 2 * QK), lambda *_: (0, 0))]
    if cm:
        bsz, length, _ = tokens.shape
        rows = length // GRID_W
        assert rows == GRID_W and GRID_W % COLS_PER_TILE == 0
        nj = GRID_W // COLS_PER_TILE
        x_view = tokens.reshape(bsz, rows, GRID_W * D_MODEL)
        grid = (bsz, nj)
        in_specs = [pl.BlockSpec((1, rows, COLS_PER_TILE * D_MODEL), lambda b, j: (b, 0, j)),
                    pl.BlockSpec((1, 1, D_MODEL), lambda b, j: (b, 0, 0)),
                    pl.BlockSpec((1, 1, D_MODEL), lambda b, j: (b, 0, 0))] + consts

        def rast(width, dtype):
            return (jax.ShapeDtypeStruct((bsz, rows, GRID_W * width), dtype),
                    pl.BlockSpec((1, rows, COLS_PER_TILE * width), lambda b, j: (b, 0, j)))

        def colm(width, dtype):
            return (jax.ShapeDtypeStruct((bsz, GRID_W, rows, width), dtype),
                    pl.BlockSpec((1, COLS_PER_TILE, rows, width), lambda b, j: (b, j, 0, 0)))
    else:
        n_tok = tokens.shape[0]
        assert n_tok % TOK_TILE == 0
        x_view = tokens
        grid = (n_tok // TOK_TILE,)
        in_specs = [pl.BlockSpec((TOK_TILE, D_MODEL), lambda i: (i, 0)),
                    pl.BlockSpec((1, 1, D_MODEL), lambda i: (0, 0, 0)),
                    pl.BlockSpec((1, 1, D_MODEL), lambda i: (0, 0, 0))] + consts

        def rast(width, dtype):
            return (jax.ShapeDtypeStruct((n_tok, width), dtype),
                    pl.BlockSpec((TOK_TILE, width), lambda i: (i, 0)))

        colm = rast

    outs = [rast(3 * QK, BF16), rast(128, F32), colm(QK, BF16), colm(2 * QK, F32),
            colm(2 * QK, BF16), colm(QK, BF16)]
    res = pl.pallas_call(
        functools.partial(_pre_kernel, cm=cm),
        grid=grid,
        in_specs=in_specs,
        out_specs=[o[1] for o in outs],
        out_shape=[o[0] for o in outs],
        compiler_params=pltpu.CompilerParams(vmem_limit_bytes=VMEM_LIMIT),
    )(x_view, shift, scale, w_pre, alog_vec, dtb_vec, lbp)
    return res


def _iota2(n):
    return (lax.broadcasted_iota(jnp.int32, (n, n), 0), lax.broadcasted_iota(jnp.int32, (n, n), 1))


def _cumsum_matrix(rev):
    ii, jj = _iota2(CHUNK)
    return jnp.where((jj >= ii) if rev else (jj <= ii), 1.0, 0.0).astype(F32)


def _unit_tri_inverse(a, rev):
    ii, jj = _iota2(CHUNK)
    row, col = (jj, ii) if rev else (ii, jj)
    eye = jnp.where(ii == jj, 1.0, 0.0).astype(F32)
    t = None
    s = 1
    while s < CHUNK:
        m = ((row ^ col) < 2 * s) & ((row & s) != 0) & ((col & s) == 0)
        a_off = jnp.where(m, a, 0.0)
        if t is None:
            t = eye - a_off
        else:
            t = t - _dot(t, _dot(a_off, t))
        s *= 2
    return t


def _gdn_kernel(ql_ref, kl_ref, vl_ref, qc_ref, kc_ref, vc_ref, abl_ref, abc_ref,
                wq_ref, wk_ref, wv_ref, o_ref,
                raw_ref, q_s, k_s, v_s, g_s, b_s, st_ref, *, n_ctx, n_lat):
    h = pl.program_id(1)
    n_all = n_ctx + n_lat
    blk = 256
    pad = 8

    def conv_segment(src_ref, w_ref, dst_ref, seg_off, seg_len, kind):
        raw_ref[0:pad, :] = jnp.zeros((pad, DH), F32)
        raw_ref[pad:pad + seg_len, :] = src_ref[0].astype(F32)
        raw_ref[pad + seg_len:2 * pad + seg_len, :] = jnp.zeros((pad, DH), F32)

        def body(i, _):
            t0 = pl.multiple_of(i * blk, blk)
            xv = raw_ref[pl.ds(t0, blk + 2 * pad), :]
            acc = jnp.zeros((blk, DH), F32)
            for j in range(CONV_K):
                sh = (CONV_K // 2 - j) % (blk + 2 * pad)
                tap = xv if sh == 0 else pltpu.roll(xv, sh, 0)
                acc = acc + tap[pad:pad + blk, :] * w_ref[j:j + 1, :]
            y = _silu(acc)
            if kind != "v":
                y = y * lax.rsqrt(jnp.sum(y * y, axis=-1, keepdims=True) + L2_EPS)
            if kind == "q":
                y = y * DH ** -0.5
            dst_ref[pl.ds(seg_off + t0, blk), :] = y
            return 0

        lax.fori_loop(0, seg_len // blk, body, 0)

    for src_c, src_l, w_ref, dst, kind in ((qc_ref, ql_ref, wq_ref, q_s, "q"),
                                            (kc_ref, kl_ref, wk_ref, k_s, "k"),
                                            (vc_ref, vl_ref, wv_ref, v_s, "v")):
        conv_segment(src_c, w_ref, dst, 0, n_ctx, kind)
        conv_segment(src_l, w_ref, dst, n_ctx, n_lat, kind)

    lane = lax.broadcasted_iota(jnp.int32, (blk, 128), 1)

    def bcast_segment(src_ref, seg_off, seg_len):
        def body(i, _):
            t0 = pl.multiple_of(i * blk, blk)
            ab = src_ref[0, pl.ds(t0, blk), :]
            for d in range(2):
                gsel = jnp.sum(jnp.where(lane == d * HEADS + h, ab, 0.0), axis=-1, keepdims=True)
                bsel = jnp.sum(jnp.where(lane == 2 * HEADS + d * HEADS + h, ab, 0.0), axis=-1, keepdims=True)
                g_s[d, pl.ds(seg_off + t0, blk), :] = jnp.broadcast_to(gsel, (blk, DH))
                b_s[d, pl.ds(seg_off + t0, blk), :] = jnp.broadcast_to(bsel, (blk, DH))
            return 0
        lax.fori_loop(0, seg_len // blk, body, 0)

    bcast_segment(abc_ref, 0, n_ctx)
    bcast_segment(abl_ref, n_ctx, n_lat)

    st_ref[...] = jnp.zeros_like(st_ref)
    o_ref[...] = jnp.zeros_like(o_ref)
    nc_ctx = n_ctx // CHUNK
    nc_all = n_all // CHUNK

    def chunk_step(c, d, write_out):
        rev = d == 1
        r0 = pl.multiple_of(c * CHUNK, CHUNK)
        q = q_s[pl.ds(r0, CHUNK), :]
        k = k_s[pl.ds(r0, CHUNK), :]
        v = v_s[pl.ds(r0, CHUNK), :]
        g = g_s[d, pl.ds(r0, CHUNK), :]
        beta = b_s[d, pl.ds(r0, CHUNK), :]
        ii, jj = _iota2(CHUNK)
        incl = (jj >= ii) if rev else (jj <= ii)
        strict = (jj > ii) if rev else (jj < ii)

        gcum = _dot_exact(_cumsum_matrix(rev), g)
        gtot = gcum[0:1, :] if rev else gcum[CHUNK - 1:CHUNK, :]
        grow = jnp.transpose(gcum)[:CHUNK, :]
        diff = gcum[:, :CHUNK] - grow
        decay = jnp.where(incl, jnp.exp(jnp.where(incl, diff, 0.0)), 0.0)
        e_g = jnp.exp(gcum)

        kb = k * beta
        kq = _dot_nt(jnp.concatenate([kb, q], axis=0), k)
        a_mat = jnp.where(strict, kq[:CHUNK] * decay, 0.0)
        attn = kq[CHUNK:] * decay
        t_inv = _unit_tri_inverse(a_mat, rev)
        sol = _dot(t_inv, jnp.concatenate([kb * e_g, v * beta], axis=1))
        w = sol[:, :DH]
        u = sol[:, DH:]
        qg = q * e_g
        kd = k * jnp.exp(gtot - gcum)

        s = st_ref[d]
        ws = _dot(jnp.concatenate([w, qg], axis=0), s)
        v_new = u - ws[:CHUNK]
        if write_out:
            o = ws[CHUNK:] + _dot(attn, v_new)
            ro = pl.multiple_of(r0 - n_ctx, CHUNK)
            o_ref[0, pl.ds(ro, CHUNK), :] += o
        st_ref[d] = s * jnp.exp(gtot) + _dot(jnp.transpose(kd), v_new)

    def ctx_body(n, _):
        chunk_step(n, 0, False)
        chunk_step(nc_ctx - 1 - n, 1, False)
        return 0

    def lat_body(n, _):
        chunk_step(nc_ctx + n, 0, True)
        chunk_step(nc_all - 1 - n, 1, True)
        return 0

    lax.fori_loop(0, nc_ctx, ctx_body, 0)
    lax.fori_loop(0, nc_all - nc_ctx, lat_body, 0)


def _gdn(aqkv_lat, aqkv_ctx, ab_lat, ab_ctx, conv_w):
    bsz, n_lat, _ = aqkv_lat.shape
    n_ctx = aqkv_ctx.shape[1]
    n_all = n_ctx + n_lat

    def stream(n, j0):
        return pl.BlockSpec((1, n, DH), lambda b, h: (b, 0, j0 + h))

    def wspec(j0):
        return pl.BlockSpec((CONV_K, DH), lambda b, h: (0, j0 + h))

    return pl.pallas_call(
        functools.partial(_gdn_kernel, n_ctx=n_ctx, n_lat=n_lat),
        grid=(bsz, HEADS),
        in_specs=[stream(n_lat, 0), stream(n_lat, HEADS), stream(n_lat, 2 * HEADS),
                  stream(n_ctx, 0), stream(n_ctx, HEADS), stream(n_ctx, 2 * HEADS),
                  pl.BlockSpec((1, n_lat, 128), lambda b, h: (b, 0, 0)),
                  pl.BlockSpec((1, n_ctx, 128), lambda b, h: (b, 0, 0)),
                  wspec(0), wspec(HEADS), wspec(2 * HEADS)],
        out_specs=pl.BlockSpec((1, n_lat, DH), lambda b, h: (b, 0, h)),
        out_shape=jax.ShapeDtypeStruct((bsz, n_lat, QK), F32),
        scratch_shapes=[pltpu.VMEM((n_lat + 16, DH), F32),
                        pltpu.VMEM((n_all, DH), F32), pltpu.VMEM((n_all, DH), F32),
                        pltpu.VMEM((n_all, DH), F32),
                        pltpu.VMEM((2, n_all, DH), F32), pltpu.VMEM((2, n_all, DH), F32),
                        pltpu.VMEM((2, DH, DH), F32)],
        compiler_params=pltpu.CompilerParams(vmem_limit_bytes=VMEM_LIMIT),
    )(aqkv_lat, aqkv_lat, aqkv_lat, aqkv_ctx, aqkv_ctx, aqkv_ctx, ab_lat, ab_ctx,
      conv_w, conv_w, conv_w)


def _gla_kernel(ql_ref, gfl_ref, gbl_ref, kfl_ref, kbl_ref, vl_ref,
                qc_ref, gfc_ref, gbc_ref, kfc_ref, kbc_ref, vc_ref,
                o_ref, st_ref, *, n_ctx, n_lat):
    st_ref[...] = jnp.zeros_like(st_ref)
    o_ref[...] = jnp.zeros_like(o_ref)

    def chunk_step(refs, c, d, write_out):
        q_ref, g_ref, k_ref, v_ref = refs
        rev = d == 1
        r0 = pl.multiple_of(c * CHUNK, CHUNK)
        q = q_ref[0, pl.ds(r0, CHUNK), :].astype(F32)
        g = g_ref[0, pl.ds(r0, CHUNK), :]
        k = k_ref[0, pl.ds(r0, CHUNK), :].astype(F32)
        v = v_ref[0, pl.ds(r0, CHUNK), :]
        ii, jj = _iota2(CHUNK)
        incl = (jj >= ii) if rev else (jj <= ii)
        row = lax.broadcasted_iota(jnp.int32, (CHUNK, DH), 0)

        gcum = _dot_exact(_cumsum_matrix(rev), g)
        if rev:
            first = row >= SUB
            m_first, m_second, bnd, last = CHUNK - SUB // 2, SUB // 2, SUB, 0
        else:
            first = row < SUB
            m_first, m_second, bnd, last = SUB // 2 - 1, CHUNK - SUB // 2 - 1, SUB - 1, CHUNK - 1
        g_m1 = gcum[m_first:m_first + 1, :]
        g_m2 = gcum[m_second:m_second + 1, :]
        g_b = gcum[bnd:bnd + 1, :]
        g_tot = gcum[last:last + 1, :]
        g_mid = jnp.where(first, g_m1, g_m2)
        qd = q * jnp.exp(gcum - g_mid)
        kd = k * jnp.exp(g_mid - gcum)
        qo = jnp.where(first, 0.0, qd * jnp.exp(g_m2 - g_b))
        ko = jnp.where(first, kd * jnp.exp(g_b - g_m1), 0.0)
        sc = _dot_nt(jnp.concatenate([qd, qo], axis=0), jnp.concatenate([kd, ko], axis=0))
        same = ((ii < SUB) == (jj < SUB))
        attn = jnp.where(incl & same, sc[:CHUNK, :CHUNK], 0.0) + sc[CHUNK:, CHUNK:]
        q_in = qd * jnp.exp(g_mid)
        k_out = kd * jnp.exp(g_tot - g_mid)

        st = st_ref[d]
        if write_out:
            o = _dot(attn, v) + _dot_nt(q_in, st)
            o_ref[0, pl.ds(r0, CHUNK), :] += o
        st_ref[d] = st * jnp.exp(g_tot) + _dot(jnp.transpose(v.astype(F32)), k_out)

    lat_f = (ql_ref, gfl_ref, kfl_ref, vl_ref)
    lat_b = (ql_ref, gbl_ref, kbl_ref, vl_ref)
    ctx_f = (qc_ref, gfc_ref, kfc_ref, vc_ref)
    ctx_b = (qc_ref, gbc_ref, kbc_ref, vc_ref)
    nc_ctx = n_ctx // CHUNK
    nc_lat = n_lat // CHUNK

    def ctx_body(n, _):
        chunk_step(ctx_f, n, 0, False)
        chunk_step(ctx_b, nc_ctx - 1 - n, 1, False)
        return 0

    def lat_body(n, _):
        chunk_step(lat_f, n, 0, True)
        chunk_step(lat_b, nc_lat - 1 - n, 1, True)
        return 0

    lax.fori_loop(0, nc_ctx, ctx_body, 0)
    lax.fori_loop(0, nc_lat, lat_body, 0)


def _gla(lat, ctx):
    bsz, n_lat, _ = lat[0].shape
    n_ctx = ctx[0].shape[1]

    def specs(n):
        one = lambda j0: pl.BlockSpec((1, n, DH), lambda b, h: (b, 0, j0 + h))
        return [one(0), one(0), one(HEADS), one(0), one(HEADS), one(0)]

    def args(t):
        q, g, k, v = t
        return [q, g, g, k, k, v]

    return pl.pallas_call(
        functools.partial(_gla_kernel, n_ctx=n_ctx, n_lat=n_lat),
        grid=(bsz, HEADS),
        in_specs=specs(n_lat) + specs(n_ctx),
        out_specs=pl.BlockSpec((1, n_lat, DH), lambda b, h: (b, 0, h)),
        out_shape=jax.ShapeDtypeStruct((bsz, n_lat, QK), F32),
        scratch_shapes=[pltpu.VMEM((2, DH, DH), F32)],
        compiler_params=pltpu.CompilerParams(vmem_limit_bytes=VMEM_LIMIT),
    )(*args(lat), *args(ctx))


def _gated_rms(o, gain, gate):
    parts = []
    for hh in range(HEADS):
        oh = o[:, hh * DH:(hh + 1) * DH]
        ms = jnp.mean(oh * oh, axis=-1, keepdims=True)
        parts.append(oh * lax.rsqrt(ms + RMS_EPS))
    return jnp.concatenate(parts, axis=1) * gain * _silu(gate)


def _post_kernel(x_ref, oa_ref, ob_ref, shift_ref, scale_ref, gate_ref, w_ref, wa_ref, wb_ref, wo_ref,
                 ga_ref, gb_ref, lng_ref, lnb_ref, out_ref):
    xt = _load_tile(x_ref, D_MODEL, True)
    u = (_layer_norm(xt) * (1.0 + scale_ref[0]) + shift_ref[0]).astype(BF16)
    oa = _load_tile(oa_ref, QK, True)
    ob = jnp.concatenate([ob_ref[0, cl] for cl in range(COLS_PER_TILE)], axis=0)

    za = jnp.dot(u, w_ref[:, 0:QK], preferred_element_type=F32)
    y_a = _dot(_gated_rms(oa, ga_ref[...], za), wa_ref[...])
    zb = jnp.dot(u, w_ref[:, QK:2 * QK], preferred_element_type=F32)
    y_b = _dot(_gated_rms(ob, gb_ref[...], zb), wb_ref[...])
    m_a = _sigmoid_pair(jnp.dot(u, w_ref[:, 2 * QK:2 * QK + D_MODEL], preferred_element_type=F32))[0]
    m_b = _sigmoid_pair(jnp.dot(u, w_ref[:, 2 * QK + D_MODEL:], preferred_element_type=F32))[0]
    sub = _dot(m_a * y_a + m_b * y_b, wo_ref[...])
    hres = DEEPNORM_ALPHA * xt + gate_ref[0] * sub
    y = _layer_norm(hres) * lng_ref[...] + lnb_ref[...]
    _store_raster(out_ref, y, D_MODEL, True)


def _post_project(x, oa, ob_cm, shift, scale, gate, w_post, w_a_out, w_b_out, w_out, a_gain, b_gain, ln_g, ln_b):
    bsz, length, _ = x.shape
    rows = length // GRID_W
    nj = GRID_W // COLS_PER_TILE
    const = lambda shape: pl.BlockSpec(shape, lambda b, j: tuple(0 for _ in shape))
    rast = lambda width: pl.BlockSpec((1, rows, COLS_PER_TILE * width), lambda b, j: (b, 0, j))
    modv = pl.BlockSpec((1, 1, D_MODEL), lambda b, j: (b, 0, 0))
    out = pl.pallas_call(
        _post_kernel,
        grid=(bsz, nj),
        in_specs=[rast(D_MODEL), rast(QK),
                  pl.BlockSpec((1, COLS_PER_TILE, rows, QK), lambda b, j: (b, j, 0, 0)),
                  modv, modv, modv,
                  const((D_MODEL, N_POST)), const((QK, D_MODEL)), const((QK, D_MODEL)),
                  const((D_MODEL, D_MODEL)), const((1, QK)), const((1, QK)),
                  const((1, D_MODEL)), const((1, D_MODEL))],
        out_specs=rast(D_MODEL),
        out_shape=jax.ShapeDtypeStruct((bsz, rows, GRID_W * D_MODEL), F32),
        compiler_params=pltpu.CompilerParams(vmem_limit_bytes=VMEM_LIMIT),
    )(x.reshape(bsz, rows, GRID_W * D_MODEL), oa.reshape(bsz, rows, GRID_W * QK),
      ob_cm.reshape(bsz, GRID_W, rows, QK), shift, scale, gate,
      w_post, w_a_out, w_b_out, w_out, a_gain, b_gain, ln_g, ln_b)
    return out.reshape(bsz, length, D_MODEL)


def kernel(x, c, ctx, c_ctx, w_mod, b_mod, w_in, conv_w, a_log, dt_bias, lb_param, a_norm_g, b_norm_g,
           w_a_out, w_b_out, w_out, ln_g, ln_b):
    assert w_mod.shape[0] == DEPTH
    bsz, length, _ = x.shape
    n_ctx = ctx.shape[1]
    f32 = lambda t: t.astype(F32)

    w = f32(w_in[0])
    o_alpha = 3 * QK
    o_agate = o_alpha + 4 * HEADS
    o_bq = o_agate + QK
    o_bgate = o_bq + 4 * QK
    o_merge = o_bgate + QK
    w_pre = jnp.concatenate([w[:, :o_alpha], jnp.pad(w[:, o_alpha:o_agate], ((0, 0), (0, 128 - 4 * HEADS))),
                             w[:, o_bq:o_bgate]], axis=1).astype(BF16)
    w_post = jnp.concatenate([w[:, o_agate:o_bq], w[:, o_bgate:]], axis=1).astype(BF16)
    alog_vec = jnp.pad(f32(a_log[0]).reshape(1, 2 * HEADS), ((0, 0), (0, 128 - 2 * HEADS)))
    dtb_vec = jnp.pad(f32(dt_bias[0]).reshape(1, 2 * HEADS), ((0, 0), (0, 128 - 2 * HEADS)))
    lbp = f32(lb_param).reshape(DEPTH + 1, 2 * QK)

    mod = _modulation(f32(c), f32(c_ctx), f32(w_mod[0]), f32(b_mod[0]))
    shift_l = mod[:bsz, None, 0:D_MODEL]
    scale_l = mod[:bsz, None, D_MODEL:2 * D_MODEL]
    gate_l = mod[:bsz, None, 2 * D_MODEL:]
    shift_c = mod[bsz:bsz + 1, None, 0:D_MODEL]
    scale_c = mod[bsz:bsz + 1, None, D_MODEL:2 * D_MODEL]

    lat = _pre_project(f32(x), shift_l, scale_l, w_pre, alog_vec, dtb_vec, lbp, cm=True)
    cx = _pre_project(f32(ctx).reshape(bsz * n_ctx, D_MODEL), shift_c, scale_c, w_pre, alog_vec, dtb_vec, lbp,
                      cm=False)
    aqkv_l = lat[0].reshape(bsz, length, 3 * QK)
    ab_l = lat[1].reshape(bsz, length, 128)
    gla_l = tuple(t.reshape(bsz, length, t.shape[-1]) for t in lat[2:])
    aqkv_c = cx[0].reshape(bsz, n_ctx, 3 * QK)
    ab_c = cx[1].reshape(bsz, n_ctx, 128)
    gla_c = tuple(t.reshape(bsz, n_ctx, t.shape[-1]) for t in cx[2:])

    oa = _gdn(aqkv_l, aqkv_c, ab_l, ab_c, f32(conv_w[0]))
    ob_cm = _gla(gla_l, gla_c)

    a_gain = jnp.tile(f32(a_norm_g[0]), HEADS).reshape(1, QK)
    b_gain = jnp.tile(f32(b_norm_g[0]), HEADS).reshape(1, QK)
    out = _post_project(f32(x), oa, ob_cm, shift_l, scale_l, gate_l, w_post,
                        f32(w_a_out[0]).astype(BF16), f32(w_b_out[0]).astype(BF16), f32(w_out[0]).astype(BF16),
                        a_gain, b_gain, f32(ln_g[0]).reshape(1, D_MODEL), f32(ln_b[0]).reshape(1, D_MODEL))
    return out.astype(x.dtype)
```

```python
import functools
import math

import jax
import jax.numpy as jnp
from jax import lax
from jax.experimental import pallas as pl
from jax.experimental.pallas import tpu as pltpu

D_MODEL = 1024
GRID_W = 64
HEADS = 4
DH = 128
QK = HEADS * DH
CONV_K = 5
CHUNK = 64
SUPER = 256
SUB = 32
DEPTH = 1
DEEPNORM_ALPHA = (2 * DEPTH) ** 0.25
LN_EPS = 1e-6
RMS_EPS = 1e-6
L2_EPS = 1e-6

TOK_TILE = 512
COLS_PER_TILE = TOK_TILE // GRID_W
N_PRE = 3 * QK + 128 + 4 * QK
N_POST = 2 * QK + 2 * D_MODEL
VMEM_LIMIT = 56 * 1024 * 1024

F32 = jnp.float32
BF16 = jnp.bfloat16


def _dot(a, b):
    return jnp.dot(a.astype(BF16), b.astype(BF16), preferred_element_type=F32)


def _dot_nt(a, b):
    return lax.dot_general(a.astype(BF16), b.astype(BF16), (((1,), (1,)), ((), ())),
                           preferred_element_type=F32)


def _dot_exact(a, b):
    return jnp.dot(a, b, preferred_element_type=F32, precision=lax.Precision.HIGHEST)


def _sigmoid_pair(x):
    t = jnp.exp(-jnp.abs(x))
    r = 1.0 / (1.0 + t)
    tr = t * r
    pos = x >= 0
    return jnp.where(pos, r, tr), jnp.where(pos, tr, r)


def _silu(x):
    return x * _sigmoid_pair(x)[0]


def _layer_norm(t):
    mu = jnp.mean(t, axis=-1, keepdims=True)
    tc = t - mu
    var = jnp.mean(tc * tc, axis=-1, keepdims=True)
    return tc * lax.rsqrt(var + LN_EPS)


def _mod_kernel(c_ref, w_ref, b_ref, o_ref):
    o_ref[...] = _dot_exact(_silu(c_ref[...]), w_ref[...]) + b_ref[...]


def _modulation(c, c_ctx, w_mod, b_mod):
    bsz = c.shape[0]
    rows = 16
    cc = jnp.zeros((rows, D_MODEL), F32).at[:bsz].set(c).at[bsz].set(c_ctx)
    nblk = 3
    out = pl.pallas_call(
        _mod_kernel,
        grid=(nblk,),
        in_specs=[pl.BlockSpec((rows, D_MODEL), lambda j: (0, 0)),
                  pl.BlockSpec((D_MODEL, D_MODEL), lambda j: (0, j)),
                  pl.BlockSpec((1, D_MODEL), lambda j: (0, j))],
        out_specs=pl.BlockSpec((rows, D_MODEL), lambda j: (0, j)),
        out_shape=jax.ShapeDtypeStruct((rows, 3 * D_MODEL), F32),
    )(cc, w_mod, b_mod.reshape(1, 3 * D_MODEL))
    return out


def _load_tile(ref, width, cm):
    if cm:
        return jnp.concatenate([ref[0, :, cl * width:(cl + 1) * width] for cl in range(COLS_PER_TILE)], axis=0)
    return ref[...]


def _store_raster(ref, val, width, cm):
    if cm:
        for cl in range(COLS_PER_TILE):
            ref[0, :, cl * width:(cl + 1) * width] = val[cl * GRID_W:(cl + 1) * GRID_W, :].astype(ref.dtype)
    else:
        ref[...] = val.astype(ref.dtype)


def _store_cm(ref, val, cm):
    if cm:
        for cl in range(COLS_PER_TILE):
            ref[0, cl] = val[cl * GRID_W:(cl + 1) * GRID_W, :].astype(ref.dtype)
    else:
        ref[...] = val.astype(ref.dtype)


def _pre_kernel(x_ref, shift_ref, scale_ref, w_ref, alog_ref, dtb_ref, lbp_ref,
                aqkv_ref, ab_ref, bq_ref, bg_ref, bk_ref, bi_ref, *, cm):
    xt = _load_tile(x_ref, D_MODEL, cm)
    u = (_layer_norm(xt) * (1.0 + scale_ref[0]) + shift_ref[0]).astype(BF16)

    for g in range(3):
        z = jnp.dot(u, w_ref[:, g * QK:(g + 1) * QK], preferred_element_type=F32)
        if cm:
            for cl in range(COLS_PER_TILE):
                aqkv_ref[0, :, cl * 3 * QK + g * QK: cl * 3 * QK + (g + 1) * QK] = (
                    z[cl * GRID_W:(cl + 1) * GRID_W, :].astype(aqkv_ref.dtype))
        else:
            aqkv_ref[:, g * QK:(g + 1) * QK] = z.astype(aqkv_ref.dtype)

    off = 3 * QK
    z = jnp.dot(u, w_ref[:, off:off + 128], preferred_element_type=F32)
    zs = z + dtb_ref[...]
    softplus = jnp.maximum(zs, 0.0) + jnp.log(1.0 + jnp.exp(-jnp.abs(zs)))
    a_g = -jnp.exp(alog_ref[...]) * softplus
    a_b = _sigmoid_pair(z)[0]
    lane = lax.broadcasted_iota(jnp.int32, z.shape, 1)
    _store_raster(ab_ref, jnp.where(lane < 2 * HEADS, a_g, a_b), 128, cm)

    off += 128
    z = jnp.dot(u, w_ref[:, off:off + QK], preferred_element_type=F32)
    _store_cm(bq_ref, _silu(z) * DH ** -0.5, cm)

    off += QK
    p0 = lbp_ref[0:1, :]
    p1 = lbp_ref[1:2, :]
    pm = jnp.maximum(p0, p1)
    e0 = jnp.exp(p0 - pm)
    lb = e0 / (e0 + jnp.exp(p1 - pm))
    for d in range(2):
        z = jnp.dot(u, w_ref[:, off + d * QK: off + (d + 1) * QK], preferred_element_type=F32)
        s_pos, s_neg = _sigmoid_pair(z)
        lbd = lb[:, d * QK:(d + 1) * QK]
        f = lbd + (1.0 - lbd) * s_pos
        gl = jnp.log(f)
        kk = (1.0 - lbd) * s_neg
        if cm:
            for cl in range(COLS_PER_TILE):
                bg_ref[0, cl, :, d * QK:(d + 1) * QK] = gl[cl * GRID_W:(cl + 1) * GRID_W, :]
                bk_ref[0, cl, :, d * QK:(d + 1) * QK] = kk[cl * GRID_W:(cl + 1) * GRID_W, :].astype(bk_ref.dtype)
        else:
            bg_ref[:, d * QK:(d + 1) * QK] = gl
            bk_ref[:, d * QK:(d + 1) * QK] = kk.astype(bk_ref.dtype)

    off += 2 * QK
    z = jnp.dot(u, w_ref[:, off:off + QK], preferred_element_type=F32)
    _store_cm(bi_ref, z, cm)


def _pre_project(tokens, shift, scale, w_pre, alog_vec, dtb_vec, lbp, *, cm):
    consts = [pl.BlockSpec((D_MODEL, N_PRE), lambda *_: (0, 0)),
              pl.BlockSpec((1, 128), lambda *_: (0, 0)),
              pl.BlockSpec((1, 128), lambda *_: (0, 0)),
              pl.BlockSpec((2, 2 * QK), lambda *_: (0, 0))]
    if cm:
        bsz, length, _ = tokens.shape
        rows = length // GRID_W
        assert rows == GRID_W and GRID_W % COLS_PER_TILE == 0
        nj = GRID_W // COLS_PER_TILE
        x_view = tokens.reshape(bsz, rows, GRID_W * D_MODEL)
        grid = (bsz, nj)
        in_specs = [pl.BlockSpec((1, rows, COLS_PER_TILE * D_MODEL), lambda b, j: (b, 0, j)),
                    pl.BlockSpec((1, 1, D_MODEL), lambda b, j: (b, 0, 0)),
                    pl.BlockSpec((1, 1, D_MODEL), lambda b, j: (b, 0, 0))] + consts

        def rast(width, dtype):
            return (jax.ShapeDtypeStruct((bsz, rows, GRID_W * width), dtype),
                    pl.BlockSpec((1, rows, COLS_PER_TILE * width), lambda b, j: (b, 0, j)))

        def colm(width, dtype):
            return (jax.ShapeDtypeStruct((bsz, GRID_W, rows, width), dtype),
                    pl.BlockSpec((1, COLS_PER_TILE, rows, width), lambda b, j: (b, j, 0, 0)))
    else:
        n_tok = tokens.shape[0]
        assert n_tok % TOK_TILE == 0
        x_view = tokens
        grid = (n_tok // TOK_TILE,)
        in_specs = [pl.BlockSpec((TOK_TILE, D_MODEL), lambda i: (i, 0)),
                    pl.BlockSpec((1, 1, D_MODEL), lambda i: (0, 0, 0)),
                    pl.BlockSpec((1, 1, D_MODEL), lambda i: (0, 0, 0))] + consts

        def rast(width, dtype):
            return (jax.ShapeDtypeStruct((n_tok, width), dtype),
                    pl.BlockSpec((TOK_TILE, width), lambda i: (i, 0)))

        colm = rast

    outs = [rast(3 * QK, BF16), rast(128, F32), colm(QK, BF16), colm(2 * QK, F32),
            colm(2 * QK, BF16), colm(QK, BF16)]
    res = pl.pallas_call(
        functools.partial(_pre_kernel, cm=cm),
        grid=grid,
        in_specs=in_specs,
        out_specs=[o[1] for o in outs],
        out_shape=[o[0] for o in outs],
        compiler_params=pltpu.CompilerParams(vmem_limit_bytes=VMEM_LIMIT),
    )(x_view, shift, scale, w_pre, alog_vec, dtb_vec, lbp)
    return res


def _iota2(n):
    return (lax.broadcasted_iota(jnp.int32, (n, n), 0), lax.broadcasted_iota(jnp.int32, (n, n), 1))


def _scan_masks(rev):
    ii, jj = _iota2(SUPER)
    same = (ii ^ jj) < CHUNK
    incl = same & ((jj >= ii) if rev else (jj <= ii))
    strict = same & ((jj > ii) if rev else (jj < ii))
    return ii, jj, same, incl, strict


def _dot_split3(m01, g):
    m = m01.astype(BF16)
    g1 = g.astype(BF16)
    r1 = g - g1.astype(F32)
    g2 = r1.astype(BF16)
    g3 = (r1 - g2.astype(F32)).astype(BF16)
    dot = lambda t: jnp.dot(m, t, preferred_element_type=F32)
    return dot(g1) + dot(g2) + dot(g3)


def _chunk_rows(t, offset, span):
    return jnp.concatenate([jnp.broadcast_to(t[span * i + offset: span * i + offset + 1, :], (span, t.shape[1]))
                            for i in range(SUPER // span)], axis=0)


def _unit_tri_inverse(a, rev):
    ii, jj = _iota2(SUPER)
    row, col = (jj, ii) if rev else (ii, jj)
    eye = jnp.where(ii == jj, 1.0, 0.0).astype(F32)
    t = None
    s = 1
    while s < CHUNK:
        m = ((row ^ col) < 2 * s) & ((row & s) != 0) & ((col & s) == 0)
        a_off = jnp.where(m, a, 0.0)
        if t is None:
            t = eye - a_off
        else:
            t = t - _dot(t, _dot(a_off, t))
        s *= 2
    return t


def _gdn_kernel(ql_ref, kl_ref, vl_ref, qc_ref, kc_ref, vc_ref, abl_ref, abc_ref,
                wq_ref, wk_ref, wv_ref, o_ref,
                raw_ref, q_s, k_s, v_s, wq_s, u_s, kdt_s, at_s, gl_s, *, n_ctx, n_lat):
    h = pl.program_id(1)
    blk = 256
    pad = 8
    assert n_ctx == SUPER and n_lat % (2 * SUPER) == 0
    ns_lat = n_lat // SUPER
    cps = SUPER // CHUNK

    def conv_segment(src_ref, w_ref, dst_ref, seg_off, seg_len, kind):
        raw_ref[0:pad, :] = jnp.zeros((pad, DH), F32)
        raw_ref[pad:pad + seg_len, :] = src_ref[0].astype(F32)
        raw_ref[pad + seg_len:2 * pad + seg_len, :] = jnp.zeros((pad, DH), F32)

        def body(i, _):
            t0 = pl.multiple_of(i * blk, blk)
            xv = raw_ref[pl.ds(t0, blk + 2 * pad), :]
            acc = jnp.zeros((blk, DH), F32)
            for j in range(CONV_K):
                sh = (CONV_K // 2 - j) % (blk + 2 * pad)
                tap = xv if sh == 0 else pltpu.roll(xv, sh, 0)
                acc = acc + tap[pad:pad + blk, :] * w_ref[j:j + 1, :]
            y = _silu(acc)
            if kind != "v":
                y = y * lax.rsqrt(jnp.sum(y * y, axis=-1, keepdims=True) + L2_EPS)
            if kind == "q":
                y = y * DH ** -0.5
            dst_ref[pl.ds(seg_off + t0, blk), :] = y
            return 0

        lax.fori_loop(0, seg_len // blk, body, 0)

    for src_c, src_l, w_ref, dst, kind in ((qc_ref, ql_ref, wq_ref, q_s, "q"),
                                            (kc_ref, kl_ref, wk_ref, k_s, "k"),
                                            (vc_ref, vl_ref, wv_ref, v_s, "v")):
        conv_segment(src_c, w_ref, dst, 0, n_ctx, kind)
        conv_segment(src_l, w_ref, dst, n_ctx, n_lat, kind)

    lane = lax.broadcasted_iota(jnp.int32, (SUPER, 128), 1)

    def prep(ab, sc):
        r0 = pl.multiple_of(sc * SUPER, SUPER)
        q = q_s[pl.ds(r0, SUPER), :]
        k = k_s[pl.ds(r0, SUPER), :]
        v = v_s[pl.ds(r0, SUPER), :]
        kq = _dot_nt(jnp.concatenate([k, q], axis=0), k)
        for d in range(2):
            rev = d == 1
            _, _, _, incl, strict = _scan_masks(rev)
            g = jnp.broadcast_to(jnp.sum(jnp.where(lane == d * HEADS + h, ab, 0.0), axis=-1, keepdims=True),
                                 (SUPER, DH))
            beta = jnp.broadcast_to(
                jnp.sum(jnp.where(lane == 2 * HEADS + d * HEADS + h, ab, 0.0), axis=-1, keepdims=True), (SUPER, DH))
            gcum = _dot_split3(jnp.where(incl, 1.0, 0.0), g)
            gtot = _chunk_rows(gcum, 0 if rev else CHUNK - 1, CHUNK)
            grow = jnp.transpose(gcum)[0:1, :]
            diff = jnp.concatenate([gcum, gcum], axis=1) - grow
            decay = jnp.where(incl, jnp.exp(jnp.where(incl, diff, 0.0)), 0.0)
            e_g = jnp.exp(gcum)
            beta2 = jnp.concatenate([beta, beta], axis=1)
            a_mat = jnp.where(strict, kq[:SUPER] * beta2 * decay, 0.0)
            attn = kq[SUPER:] * decay
            t_inv = _unit_tri_inverse(a_mat, rev)
            kb = k * beta
            sol = _dot(t_inv, jnp.concatenate([kb * e_g, v * beta], axis=1))
            qg = q * e_g
            kdt = jnp.transpose(k * jnp.exp(gtot - gcum))
            for c in range(cps):
                rs = slice(c * CHUNK, (c + 1) * CHUNK)
                wq_s[d, sc * cps + c] = jnp.concatenate([sol[rs, :DH], qg[rs, :]], axis=0).astype(BF16)
                at_s[d, sc, rs, :] = attn[rs, rs].astype(BF16)
            u_s[d, pl.ds(r0, SUPER), :] = sol[:, DH:]
            kdt_s[d, sc] = kdt.astype(BF16)
            gl_s[d, sc] = jnp.concatenate(
                [jnp.exp(gtot[c * CHUNK:c * CHUNK + 1, :]) for c in range(cps)] + [jnp.zeros((8 - cps, DH), F32)],
                axis=0)

    prep(abc_ref[0], 0)

    def prep_body(n, _):
        prep(abl_ref[0, pl.ds(pl.multiple_of(n * SUPER, SUPER), SUPER), :], n + 1)
        return 0

    lax.fori_loop(0, ns_lat, prep_body, 0)

    def scan_super(sc, d, st, mode):
        r0 = pl.multiple_of(sc * SUPER, SUPER)
        gl = gl_s[d, sc]
        for c in (range(cps - 1, -1, -1) if d == 1 else range(cps)):
            rs = slice(c * CHUNK, (c + 1) * CHUNK)
            ws = jnp.dot(wq_s[d, sc * cps + c], st.astype(BF16), preferred_element_type=F32)
            v_new = (u_s[d, pl.ds(r0 + c * CHUNK, CHUNK), :] - ws[:CHUNK]).astype(BF16)
            if mode is not None:
                o = ws[CHUNK:] + jnp.dot(at_s[d, sc, rs, :], v_new, preferred_element_type=F32)
                ro = pl.multiple_of(r0 - n_ctx + c * CHUNK, CHUNK)
                if mode == "set":
                    o_ref[0, pl.ds(ro, CHUNK), :] = o
                else:
                    o_ref[0, pl.ds(ro, CHUNK), :] += o
            st = st * gl[c:c + 1, :] + jnp.dot(kdt_s[d, sc, :, rs], v_new, preferred_element_type=F32)
        return st

    zero = jnp.zeros((DH, DH), F32)
    st_f = scan_super(0, 0, zero, None)
    st_b = scan_super(0, 1, zero, None)

    def make_body(mode):
        def body(n, carry):
            sf, sb = carry
            return scan_super(1 + n, 0, sf, mode), scan_super(ns_lat - n, 1, sb, mode)
        return body

    carry = lax.fori_loop(0, ns_lat // 2, make_body("set"), (st_f, st_b))
    lax.fori_loop(ns_lat // 2, ns_lat, make_body("add"), carry)


def _gdn(aqkv_lat, aqkv_ctx, ab_lat, ab_ctx, conv_w):
    bsz, n_lat, _ = aqkv_lat.shape
    n_ctx = aqkv_ctx.shape[1]
    n_all = n_ctx + n_lat
    ns = n_all // SUPER

    def stream(n, j0):
        return pl.BlockSpec((1, n, DH), lambda b, h: (b, 0, j0 + h))

    def wspec(j0):
        return pl.BlockSpec((CONV_K, DH), lambda b, h: (0, j0 + h))

    return pl.pallas_call(
        functools.partial(_gdn_kernel, n_ctx=n_ctx, n_lat=n_lat),
        grid=(bsz, HEADS),
        in_specs=[stream(n_lat, 0), stream(n_lat, HEADS), stream(n_lat, 2 * HEADS),
                  stream(n_ctx, 0), stream(n_ctx, HEADS), stream(n_ctx, 2 * HEADS),
                  pl.BlockSpec((1, n_lat, 128), lambda b, h: (b, 0, 0)),
                  pl.BlockSpec((1, n_ctx, 128), lambda b, h: (b, 0, 0)),
                  wspec(0), wspec(HEADS), wspec(2 * HEADS)],
        out_specs=pl.BlockSpec((1, n_lat, DH), lambda b, h: (b, 0, h)),
        out_shape=jax.ShapeDtypeStruct((bsz, n_lat, QK), F32),
        scratch_shapes=[pltpu.VMEM((n_lat + 16, DH), F32),
                        pltpu.VMEM((n_all, DH), F32), pltpu.VMEM((n_all, DH), F32),
                        pltpu.VMEM((n_all, DH), F32),
                        pltpu.VMEM((2, n_all // CHUNK, 2 * CHUNK, DH), BF16),
                        pltpu.VMEM((2, n_all, DH), F32),
                        pltpu.VMEM((2, ns, DH, SUPER), BF16),
                        pltpu.VMEM((2, ns, SUPER, CHUNK), BF16),
                        pltpu.VMEM((2, ns, 8, DH), F32)],
        compiler_params=pltpu.CompilerParams(vmem_limit_bytes=VMEM_LIMIT),
    )(aqkv_lat, aqkv_lat, aqkv_lat, aqkv_ctx, aqkv_ctx, aqkv_ctx, ab_lat, ab_ctx,
      conv_w, conv_w, conv_w)


def _gla_kernel(ql_ref, gfl_ref, gbl_ref, kfl_ref, kbl_ref, vl_ref,
                qc_ref, gfc_ref, gbc_ref, kfc_ref, kbc_ref, vc_ref,
                o_ref, *, n_ctx, n_lat):
    assert n_ctx == SUPER and n_lat % (2 * SUPER) == 0
    ns_lat = n_lat // SUPER
    cps = SUPER // CHUNK

    def scan_super(refs, sc, d, st, mode):
        q_ref, g_ref, k_ref, v_ref = refs
        rev = d == 1
        r0 = pl.multiple_of(sc * SUPER, SUPER)
        g = g_ref[0, pl.ds(r0, SUPER), :]
        k = k_ref[0, pl.ds(r0, SUPER), :].astype(F32)
        v = v_ref[0, pl.ds(r0, SUPER), :]
        ii, jj, same, incl, _ = _scan_masks(rev)
        gcum = _dot_split3(jnp.where(incl, 1.0, 0.0), g)
        g_mid = _chunk_rows(gcum, SUB // 2 if rev else SUB // 2 - 1, SUB)
        g_tot = _chunk_rows(gcum, 0 if rev else CHUNK - 1, CHUNK)
        kd = k * jnp.exp(g_mid - gcum)
        k_out = kd * jnp.exp(g_tot - g_mid)
        if mode is not None:
            q = q_ref[0, pl.ds(r0, SUPER), :].astype(F32)
            row = lax.broadcasted_iota(jnp.int32, (SUPER, DH), 0) & (CHUNK - 1)
            first = (row >= SUB) if rev else (row < SUB)
            g_bnd = _chunk_rows(gcum, SUB if rev else SUB - 1, CHUNK)
            qd = q * jnp.exp(gcum - g_mid)
            e_b = jnp.exp(jnp.where(first, g_bnd - g_mid, g_mid - g_bnd))
            qo = jnp.where(first, 0.0, qd * e_b)
            ko = jnp.where(first, kd * e_b, 0.0)
            sub_diag = incl & ((ii ^ jj) < SUB)
            attn = jnp.where(sub_diag, _dot_nt(qd, kd), 0.0) + jnp.where(same, _dot_nt(qo, ko), 0.0)
            o_intra = _dot(attn, v)
            q_in = qd * jnp.exp(g_mid)
        outs = [None] * cps
        for c in (range(cps - 1, -1, -1) if rev else range(cps)):
            rs = slice(c * CHUNK, (c + 1) * CHUNK)
            if mode is not None:
                outs[c] = o_intra[rs] + _dot_nt(q_in[rs], st)
            gl = jnp.exp(g_tot[c * CHUNK:c * CHUNK + 1, :])
            st = st * gl + _dot(jnp.transpose(v[rs].astype(F32)), k_out[rs])
        if mode == "set":
            o_ref[0, pl.ds(r0, SUPER), :] = jnp.concatenate(outs, axis=0)
        elif mode == "add":
            o_ref[0, pl.ds(r0, SUPER), :] += jnp.concatenate(outs, axis=0)
        return st

    lat_f = (ql_ref, gfl_ref, kfl_ref, vl_ref)
    lat_b = (ql_ref, gbl_ref, kbl_ref, vl_ref)
    zero = jnp.zeros((DH, DH), F32)
    st_f = scan_super((qc_ref, gfc_ref, kfc_ref, vc_ref), 0, 0, zero, None)
    st_b = scan_super((qc_ref, gbc_ref, kbc_ref, vc_ref), 0, 1, zero, None)

    def make_body(mode):
        def body(n, carry):
            sf, sb = carry
            return scan_super(lat_f, n, 0, sf, mode), scan_super(lat_b, ns_lat - 1 - n, 1, sb, mode)
        return body

    carry = lax.fori_loop(0, ns_lat // 2, make_body("set"), (st_f, st_b))
    lax.fori_loop(ns_lat // 2, ns_lat, make_body("add"), carry)


def _gla(lat, ctx):
    bsz, n_lat, _ = lat[0].shape
    n_ctx = ctx[0].shape[1]

    def specs(n):
        one = lambda j0: pl.BlockSpec((1, n, DH), lambda b, h: (b, 0, j0 + h))
        return [one(0), one(0), one(HEADS), one(0), one(HEADS), one(0)]

    def args(t):
        q, g, k, v = t
        return [q, g, g, k, k, v]

    return pl.pallas_call(
        functools.partial(_gla_kernel, n_ctx=n_ctx, n_lat=n_lat),
        grid=(bsz, HEADS),
        in_specs=specs(n_lat) + specs(n_ctx),
        out_specs=pl.BlockSpec((1, n_lat, DH), lambda b, h: (b, 0, h)),
        out_shape=jax.ShapeDtypeStruct((bsz, n_lat, QK), F32),
        compiler_params=pltpu.CompilerParams(vmem_limit_bytes=VMEM_LIMIT),
    )(*args(lat), *args(ctx))


def _gated_rms(o, gain, gate):
    parts = []
    for hh in range(HEADS):
        oh = o[:, hh * DH:(hh + 1) * DH]
        ms = jnp.mean(oh * oh, axis=-1, keepdims=True)
        parts.append(oh * lax.rsqrt(ms + RMS_EPS))
    return jnp.concatenate(parts, axis=1) * gain * _silu(gate)


def _post_kernel(x_ref, oa_ref, ob_ref, shift_ref, scale_ref, gate_ref, w_ref, wa_ref, wb_ref, wo_ref,
                 ga_ref, gb_ref, lng_ref, lnb_ref, out_ref):
    xt = _load_tile(x_ref, D_MODEL, True)
    u = (_layer_norm(xt) * (1.0 + scale_ref[0]) + shift_ref[0]).astype(BF16)
    oa = _load_tile(oa_ref, QK, True)
    ob = jnp.concatenate([ob_ref[0, cl] for cl in range(COLS_PER_TILE)], axis=0)

    za = jnp.dot(u, w_ref[:, 0:QK], preferred_element_type=F32)
    y_a = _dot(_gated_rms(oa, ga_ref[...], za), wa_ref[...])
    zb = jnp.dot(u, w_ref[:, QK:2 * QK], preferred_element_type=F32)
    y_b = _dot(_gated_rms(ob, gb_ref[...], zb), wb_ref[...])
    m_a = _sigmoid_pair(jnp.dot(u, w_ref[:, 2 * QK:2 * QK + D_MODEL], preferred_element_type=F32))[0]
    m_b = _sigmoid_pair(jnp.dot(u, w_ref[:, 2 * QK + D_MODEL:], preferred_element_type=F32))[0]
    sub = _dot(m_a * y_a + m_b * y_b, wo_ref[...])
    hres = DEEPNORM_ALPHA * xt + gate_ref[0] * sub
    y = _layer_norm(hres) * lng_ref[...] + lnb_ref[...]
    _store_raster(out_ref, y, D_MODEL, True)


def _post_project(x, oa, ob_cm, shift, scale, gate, w_post, w_a_out, w_b_out, w_out, a_gain, b_gain, ln_g, ln_b):
    bsz, length, _ = x.shape
    rows = length // GRID_W
    nj = GRID_W // COLS_PER_TILE
    const = lambda shape: pl.BlockSpec(shape, lambda b, j: tuple(0 for _ in shape))
    rast = lambda width: pl.BlockSpec((1, rows, COLS_PER_TILE * width), lambda b, j: (b, 0, j))
    modv = pl.BlockSpec((1, 1, D_MODEL), lambda b, j: (b, 0, 0))
    out = pl.pallas_call(
        _post_kernel,
        grid=(bsz, nj),
        in_specs=[rast(D_MODEL), rast(QK),
                  pl.BlockSpec((1, COLS_PER_TILE, rows, QK), lambda b, j: (b, j, 0, 0)),
                  modv, modv, modv,
                  const((D_MODEL, N_POST)), const((QK, D_MODEL)), const((QK, D_MODEL)),
                  const((D_MODEL, D_MODEL)), const((1, QK)), const((1, QK)),
                  const((1, D_MODEL)), const((1, D_MODEL))],
        out_specs=rast(D_MODEL),
        out_shape=jax.ShapeDtypeStruct((bsz, rows, GRID_W * D_MODEL), F32),
        compiler_params=pltpu.CompilerParams(vmem_limit_bytes=VMEM_LIMIT),
    )(x.reshape(bsz, rows, GRID_W * D_MODEL), oa.reshape(bsz, rows, GRID_W * QK),
      ob_cm.reshape(bsz, GRID_W, rows, QK), shift, scale, gate,
      w_post, w_a_out, w_b_out, w_out, a_gain, b_gain, ln_g, ln_b)
    return out.reshape(bsz, length, D_MODEL)


def kernel(x, c, ctx, c_ctx, w_mod, b_mod, w_in, conv_w, a_log, dt_bias, lb_param, a_norm_g, b_norm_g,
           w_a_out, w_b_out, w_out, ln_g, ln_b):
    assert w_mod.shape[0] == DEPTH
    bsz, length, _ = x.shape
    n_ctx = ctx.shape[1]
    f32 = lambda t: t.astype(F32)

    w = f32(w_in[0])
    o_alpha = 3 * QK
    o_agate = o_alpha + 4 * HEADS
    o_bq = o_agate + QK
    o_bgate = o_bq + 4 * QK
    o_merge = o_bgate + QK
    w_pre = jnp.concatenate([w[:, :o_alpha], jnp.pad(w[:, o_alpha:o_agate], ((0, 0), (0, 128 - 4 * HEADS))),
                             w[:, o_bq:o_bgate]], axis=1).astype(BF16)
    w_post = jnp.concatenate([w[:, o_agate:o_bq], w[:, o_bgate:]], axis=1).astype(BF16)
    alog_vec = jnp.pad(f32(a_log[0]).reshape(1, 2 * HEADS), ((0, 0), (0, 128 - 2 * HEADS)))
    dtb_vec = jnp.pad(f32(dt_bias[0]).reshape(1, 2 * HEADS), ((0, 0), (0, 128 - 2 * HEADS)))
    lbp = f32(lb_param).reshape(DEPTH + 1, 2 * QK)

    mod = _modulation(f32(c), f32(c_ctx), f32(w_mod[0]), f32(b_mod[0]))
    shift_l = mod[:bsz, None, 0:D_MODEL]
    scale_l = mod[:bsz, None, D_MODEL:2 * D_MODEL]
    gate_l = mod[:bsz, None, 2 * D_MODEL:]
    shift_c = mod[bsz:bsz + 1, None, 0:D_MODEL]
    scale_c = mod[bsz:bsz + 1, None, D_MODEL:2 * D_MODEL]

    lat = _pre_project(f32(x), shift_l, scale_l, w_pre, alog_vec, dtb_vec, lbp, cm=True)
    cx = _pre_project(f32(ctx).reshape(bsz * n_ctx, D_MODEL), shift_c, scale_c, w_pre, alog_vec, dtb_vec, lbp,
                      cm=False)
    aqkv_l = lat[0].reshape(bsz, length, 3 * QK)
    ab_l = lat[1].reshape(bsz, length, 128)
    gla_l = tuple(t.reshape(bsz, length, t.shape[-1]) for t in lat[2:])
    aqkv_c = cx[0].reshape(bsz, n_ctx, 3 * QK)
    ab_c = cx[1].reshape(bsz, n_ctx, 128)
    gla_c = tuple(t.reshape(bsz, n_ctx, t.shape[-1]) for t in cx[2:])

    oa = _gdn(aqkv_l, aqkv_c, ab_l, ab_c, f32(conv_w[0]))
    ob_cm = _gla(gla_l, gla_c)

    a_gain = jnp.tile(f32(a_norm_g[0]), HEADS).reshape(1, QK)
    b_gain = jnp.tile(f32(b_norm_g[0]), HEADS).reshape(1, QK)
    out = _post_project(f32(x), oa, ob_cm, shift_l, scale_l, gate_l, w_post,
                        f32(w_a_out[0]).astype(BF16), f32(w_b_out[0]).astype(BF16), f32(w_out[0]).astype(BF16),
                        a_gain, b_gain, f32(ln_g[0]).reshape(1, D_MODEL), f32(ln_b[0]).reshape(1, D_MODEL))
    return out.astype(x.dtype)
```

```python
import functools

import jax
import jax.numpy as jnp
from jax import lax
from jax.experimental import pallas as pl
from jax.experimental.pallas import tpu as pltpu

D_MODEL = 1024
GRID_W = 64
HEADS = 4
DH = 128
QK = HEADS * DH
CONV_K = 5
CHUNK = 64
SUPER = 256
SUB = 32
HP = 2
DEPTH = 1
DEEPNORM_ALPHA = (2 * DEPTH) ** 0.25
LN_EPS = 1e-6
RMS_EPS = 1e-6
L2_EPS = 1e-6

TOK_TILE = 512
COLS_PER_TILE = TOK_TILE // GRID_W
N_PRE = 3 * QK + 128 + 4 * QK
N_POST = 2 * QK + 2 * D_MODEL
VMEM_LIMIT = 56 * 1024 * 1024

F32 = jnp.float32
BF16 = jnp.bfloat16


def _dot(a, b):
    return jnp.dot(a.astype(BF16), b.astype(BF16), preferred_element_type=F32)


def _dot_nt(a, b):
    return lax.dot_general(a.astype(BF16), b.astype(BF16), (((1,), (1,)), ((), ())),
                           preferred_element_type=F32)


def _dot_exact(a, b):
    return jnp.dot(a, b, preferred_element_type=F32, precision=lax.Precision.HIGHEST)


def _sigmoid(x):
    return 1.0 / (1.0 + jnp.exp(-x))


def _sigmoid_pair(x):
    t = jnp.exp(-jnp.abs(x))
    r = 1.0 / (1.0 + t)
    tr = t * r
    pos = x >= 0
    return jnp.where(pos, r, tr), jnp.where(pos, tr, r)


def _silu(x):
    return x * _sigmoid(x)


def _layer_norm(t):
    mu = jnp.mean(t, axis=-1, keepdims=True)
    tc = t - mu
    var = jnp.mean(tc * tc, axis=-1, keepdims=True)
    return tc * lax.rsqrt(var + LN_EPS)


def _mod_kernel(c_ref, w_ref, b_ref, o_ref):
    o_ref[...] = _dot_exact(_silu(c_ref[...]), w_ref[...]) + b_ref[...]


def _modulation(c, c_ctx, w_mod, b_mod):
    bsz = c.shape[0]
    rows = 16
    cc = jnp.zeros((rows, D_MODEL), F32).at[:bsz].set(c).at[bsz].set(c_ctx)
    nblk = 3
    out = pl.pallas_call(
        _mod_kernel,
        grid=(nblk,),
        in_specs=[pl.BlockSpec((rows, D_MODEL), lambda j: (0, 0)),
                  pl.BlockSpec((D_MODEL, D_MODEL), lambda j: (0, j)),
                  pl.BlockSpec((1, D_MODEL), lambda j: (0, j))],
        out_specs=pl.BlockSpec((rows, D_MODEL), lambda j: (0, j)),
        out_shape=jax.ShapeDtypeStruct((rows, 3 * D_MODEL), F32),
        name="mod",
    )(cc, w_mod, b_mod.reshape(1, 3 * D_MODEL))
    return out


def _load_raster(ref, grid_tile):
    return ref[0].reshape(TOK_TILE, ref.shape[-1]) if grid_tile else ref[...]


def _load_colmajor(ref):
    return jnp.concatenate([ref[0, :, cl, :] for cl in range(COLS_PER_TILE)], axis=0)


def _store_raster(ref, val, grid_tile, lanes=slice(None)):
    if grid_tile:
        ref[0, :, :, lanes] = val.reshape(GRID_W, COLS_PER_TILE, val.shape[-1]).astype(ref.dtype)
    else:
        ref[:, lanes] = val.astype(ref.dtype)


def _store_colmajor(ref, val, grid_tile, lanes=slice(None)):
    if grid_tile:
        for cl in range(COLS_PER_TILE):
            ref[0, cl, :, lanes] = val[cl * GRID_W:(cl + 1) * GRID_W, :].astype(ref.dtype)
    else:
        ref[:, lanes] = val.astype(ref.dtype)


def _pre_kernel(x_ref, shift_ref, scale_ref, w_ref, alog_ref, dtb_ref, lbp_ref,
                aqkv_ref, ab_ref, bq_ref, bg_ref, bk_ref, bi_ref, *, cm):
    modulate = lambda t: (_layer_norm(t) * (1.0 + scale_ref[0]) + shift_ref[0]).astype(BF16)
    u = modulate(_load_raster(x_ref, cm))
    u_b = modulate(_load_colmajor(x_ref)) if cm else u

    for g in range(3):
        z = jnp.dot(u, w_ref[:, g * QK:(g + 1) * QK], preferred_element_type=F32)
        _store_raster(aqkv_ref, z, cm, slice(g * QK, (g + 1) * QK))

    off = 3 * QK
    z = jnp.dot(u, w_ref[:, off:off + 128], preferred_element_type=F32)
    zs = z + dtb_ref[...]
    softplus = jnp.maximum(zs, 0.0) + jnp.log(1.0 + jnp.exp(-jnp.abs(zs)))
    a_g = -jnp.exp(alog_ref[...]) * softplus
    a_b = _sigmoid(z)
    lane = lax.broadcasted_iota(jnp.int32, z.shape, 1)
    _store_raster(ab_ref, jnp.where(lane < 2 * HEADS, a_g, a_b), cm)

    off += 128
    z = jnp.dot(u_b, w_ref[:, off:off + QK], preferred_element_type=F32)
    _store_colmajor(bq_ref, _silu(z) * DH ** -0.5, cm)

    off += QK
    p0 = lbp_ref[0:1, :]
    p1 = lbp_ref[1:2, :]
    pm = jnp.maximum(p0, p1)
    e0 = jnp.exp(p0 - pm)
    lb = e0 / (e0 + jnp.exp(p1 - pm))
    for d in range(2):
        z = jnp.dot(u_b, w_ref[:, off + d * QK: off + (d + 1) * QK], preferred_element_type=F32)
        s_pos, s_neg = _sigmoid_pair(z)
        lbd = lb[:, d * QK:(d + 1) * QK]
        _store_colmajor(bg_ref, jnp.log(lbd + (1.0 - lbd) * s_pos), cm, slice(d * QK, (d + 1) * QK))
        _store_colmajor(bk_ref, (1.0 - lbd) * s_neg, cm, slice(d * QK, (d + 1) * QK))

    off += 2 * QK
    z = jnp.dot(u_b, w_ref[:, off:off + QK], preferred_element_type=F32)
    _store_colmajor(bi_ref, z, cm)


def _pre_project(tokens, shift, scale, w_pre, alog_vec, dtb_vec, lbp, *, cm):
    consts = [pl.BlockSpec((D_MODEL, N_PRE), lambda *_: (0, 0)),
              pl.BlockSpec((1, 128), lambda *_: (0, 0)),
              pl.BlockSpec((1, 128), lambda *_: (0, 0)),
              pl.BlockSpec((2, 2 * QK), lambda *_: (0, 0))]
    if cm:
        bsz, length, _ = tokens.shape
        rows = length // GRID_W
        assert rows == GRID_W and GRID_W % COLS_PER_TILE == 0
        nj = GRID_W // COLS_PER_TILE
        x_view = tokens.reshape(bsz, rows, GRID_W, D_MODEL)
        grid = (bsz, nj)
        in_specs = [pl.BlockSpec((1, rows, COLS_PER_TILE, D_MODEL), lambda b, j: (b, 0, j, 0)),
                    pl.BlockSpec((1, 1, D_MODEL), lambda b, j: (b, 0, 0)),
                    pl.BlockSpec((1, 1, D_MODEL), lambda b, j: (b, 0, 0))] + consts

        def rast(width, dtype):
            return (jax.ShapeDtypeStruct((bsz, rows, GRID_W, width), dtype),
                    pl.BlockSpec((1, rows, COLS_PER_TILE, width), lambda b, j: (b, 0, j, 0)))

        def colm(width, dtype):
            return (jax.ShapeDtypeStruct((bsz, GRID_W, rows, width), dtype),
                    pl.BlockSpec((1, COLS_PER_TILE, rows, width), lambda b, j: (b, j, 0, 0)))
    else:
        n_tok = tokens.shape[0]
        assert n_tok % TOK_TILE == 0
        x_view = tokens
        grid = (n_tok // TOK_TILE,)
        in_specs = [pl.BlockSpec((TOK_TILE, D_MODEL), lambda i: (i, 0)),
                    pl.BlockSpec((1, 1, D_MODEL), lambda i: (0, 0, 0)),
                    pl.BlockSpec((1, 1, D_MODEL), lambda i: (0, 0, 0))] + consts

        def rast(width, dtype):
            return (jax.ShapeDtypeStruct((n_tok, width), dtype),
                    pl.BlockSpec((TOK_TILE, width), lambda i: (i, 0)))

        colm = rast

    outs = [rast(3 * QK, BF16), rast(128, F32), colm(QK, BF16), colm(2 * QK, F32),
            colm(2 * QK, BF16), colm(QK, BF16)]
    res = pl.pallas_call(
        functools.partial(_pre_kernel, cm=cm),
        grid=grid,
        in_specs=in_specs,
        out_specs=[o[1] for o in outs],
        out_shape=[o[0] for o in outs],
        compiler_params=pltpu.CompilerParams(vmem_limit_bytes=VMEM_LIMIT),
        name="pre_lat" if cm else "pre_ctx",
    )(x_view, shift, scale, w_pre, alog_vec, dtb_vec, lbp)
    return res


def _iota2(n):
    return (lax.broadcasted_iota(jnp.int32, (n, n), 0), lax.broadcasted_iota(jnp.int32, (n, n), 1))


def _scan_masks(rev):
    ii, jj = _iota2(SUPER)
    same = (ii ^ jj) < CHUNK
    incl = same & ((jj >= ii) if rev else (jj <= ii))
    strict = same & ((jj > ii) if rev else (jj < ii))
    return ii, jj, same, incl, strict


def _dot_split3(m01, g):
    m = m01.astype(BF16)
    g1 = g.astype(BF16)
    r1 = g - g1.astype(F32)
    g2 = r1.astype(BF16)
    g3 = (r1 - g2.astype(F32)).astype(BF16)
    dot = lambda t: jnp.dot(m, t, preferred_element_type=F32)
    return dot(g1) + dot(g2) + dot(g3)


def _chunk_rows(t, offset, span):
    return jnp.concatenate([jnp.broadcast_to(t[span * i + offset: span * i + offset + 1, :], (span, t.shape[1]))
                            for i in range(SUPER // span)], axis=0)


def _gdn_kernel(ql_ref, kl_ref, vl_ref, qc_ref, kc_ref, vc_ref, abl_ref, abc_ref,
                wq_ref, wk_ref, wv_ref, o_ref,
                raw_ref, q_s, k_s, v_s, wq_s, u_s, kdt_s, at_s, gl_s, st_s, *, n_ctx, n_lat):
    head0 = pl.program_id(1) * HP
    blk = 256
    pad = 8
    assert n_ctx == SUPER and n_lat % (2 * SUPER) == 0
    ns_lat = n_lat // SUPER
    cps = SUPER // CHUNK
    width = HP * DH

    def conv_segment(src_ref, w_ref, dst_ref, seg_off, seg_len, kind):
        raw_ref[0:pad, :] = jnp.zeros((pad, width), F32)
        raw_ref[pad:pad + seg_len, :] = src_ref[0].astype(F32)
        raw_ref[pad + seg_len:2 * pad + seg_len, :] = jnp.zeros((pad, width), F32)

        def body(i, _):
            t0 = pl.multiple_of(i * blk, blk)
            xv = raw_ref[pl.ds(t0, blk + 2 * pad), :]
            acc = jnp.zeros((blk, width), F32)
            for j in range(CONV_K):
                sh = (CONV_K // 2 - j) % (blk + 2 * pad)
                tap = xv if sh == 0 else pltpu.roll(xv, sh, 0)
                acc = acc + tap[pad:pad + blk, :] * w_ref[j:j + 1, :]
            y = _silu(acc)
            for hh in range(HP):
                yh = y[:, hh * DH:(hh + 1) * DH]
                if kind != "v":
                    yh = yh * lax.rsqrt(jnp.sum(yh * yh, axis=-1, keepdims=True) + L2_EPS)
                if kind == "q":
                    yh = yh * DH ** -0.5
                dst_ref[pl.ds(seg_off + t0, blk), hh * DH:(hh + 1) * DH] = yh
            return 0

        lax.fori_loop(0, seg_len // blk, body, 0)

    for src_c, src_l, w_ref, dst, kind in ((qc_ref, ql_ref, wq_ref, q_s, "q"),
                                            (kc_ref, kl_ref, wk_ref, k_s, "k"),
                                            (vc_ref, vl_ref, wv_ref, v_s, "v")):
        conv_segment(src_c, w_ref, dst, 0, n_ctx, kind)
        conv_segment(src_l, w_ref, dst, n_ctx, n_lat, kind)

    lane = lax.broadcasted_iota(jnp.int32, (SUPER, 128), 1)

    def prep(items, slot):
        idx = range(len(items))
        masks = {d: _scan_masks(d == 1) for d in {it[1] for it in items}}
        incl = [masks[it[1]][3] for it in items]
        strict = [masks[it[1]][4] for it in items]
        revs = [it[1] == 1 for it in items]
        r0 = [pl.multiple_of(it[3] * SUPER, SUPER) for it in items]
        hs = [slice(it[0] * DH, (it[0] + 1) * DH) for it in items]
        q = [q_s[pl.ds(r0[i], SUPER), hs[i]] for i in idx]
        k = [k_s[pl.ds(r0[i], SUPER), hs[i]] for i in idx]
        v = [v_s[pl.ds(r0[i], SUPER), hs[i]] for i in idx]
        kq = [_dot_nt(jnp.concatenate([k[i], q[i]], axis=0), k[i]) for i in idx]
        cols = [it[1] * HEADS + head0 + it[0] for it in items]
        pick = lambda ab, col: jnp.broadcast_to(
            jnp.sum(jnp.where(lane == col, ab, 0.0), axis=-1, keepdims=True), (SUPER, DH))
        g = [pick(items[i][2], cols[i]) for i in idx]
        beta = [pick(items[i][2], 2 * HEADS + cols[i]) for i in idx]
        gcum = [_dot_split3(jnp.where(incl[i], 1.0, 0.0), g[i]) for i in idx]
        yield
        gtot = [_chunk_rows(gcum[i], 0 if revs[i] else CHUNK - 1, CHUNK) for i in idx]
        grow = [jnp.transpose(gcum[i])[0:1, :] for i in idx]
        decay = [jnp.where(incl[i], jnp.exp(jnp.where(
            incl[i], jnp.concatenate([gcum[i], gcum[i]], axis=1) - grow[i], 0.0)), 0.0) for i in idx]
        e_g = [jnp.exp(gcum[i]) for i in idx]
        a_mat = [jnp.where(strict[i], kq[i][:SUPER] * jnp.concatenate([beta[i], beta[i]], axis=1) * decay[i], 0.0)
                 for i in idx]
        attn = [kq[i][SUPER:] * decay[i] for i in idx]
        for c in range(cps):
            rs = slice(c * CHUNK, (c + 1) * CHUNK)
            for i in idx:
                at_s[slot, items[i][0], items[i][1], rs, :] = attn[i][rs, rs].astype(BF16)
        ii, jj = _iota2(SUPER)
        eye = jnp.where(ii == jj, 1.0, 0.0).astype(F32)
        t = None
        s = 1
        while s < CHUNK:
            a_off = []
            for i in idx:
                row, col = (jj, ii) if revs[i] else (ii, jj)
                m = ((row ^ col) < 2 * s) & ((row & s) != 0) & ((col & s) == 0)
                a_off.append(jnp.where(m, a_mat[i], 0.0))
            if t is None:
                t = [eye - a_off[i] for i in idx]
            else:
                p = [_dot(a_off[i], t[i]) for i in idx]
                yield
                t = [t[i] - _dot(t[i], p[i]) for i in idx]
                yield
            s *= 2
        sol = [_dot(t[i], jnp.concatenate([k[i] * beta[i] * e_g[i], v[i] * beta[i]], axis=1)) for i in idx]
        qg = [q[i] * e_g[i] for i in idx]
        kdt = [jnp.transpose(k[i] * jnp.exp(gtot[i] - gcum[i])) for i in idx]
        yield
        for i in idx:
            hh, d = items[i][0], items[i][1]
            for c in range(cps):
                rs = slice(c * CHUNK, (c + 1) * CHUNK)
                wq_s[slot, hh, d, c] = jnp.concatenate([sol[i][rs, :DH], qg[i][rs, :]], axis=0).astype(BF16)
            u_s[slot, hh, d] = sol[i][:, DH:]
            kdt_s[slot, hh, d] = kdt[i].astype(BF16)
            gl_s[slot, hh, d] = jnp.concatenate(
                [jnp.exp(gtot[i][c * CHUNK:c * CHUNK + 1, :]) for c in range(cps)]
                + [jnp.zeros((8 - cps, DH), F32)], axis=0)

    def scan(items, slot, mode):
        idx = range(len(items))
        st = [st_s[it[0], it[1]] for it in items]
        gl = [gl_s[slot, it[0], it[1]] for it in items]
        for step_c in range(cps):
            cs = [(cps - 1 - step_c) if it[1] == 1 else step_c for it in items]
            rs = [slice(c * CHUNK, (c + 1) * CHUNK) for c in cs]
            ws = [jnp.dot(wq_s[slot, items[i][0], items[i][1], cs[i]], st[i].astype(BF16),
                          preferred_element_type=F32) for i in idx]
            yield
            v_new = [(u_s[slot, items[i][0], items[i][1], rs[i], :] - ws[i][:CHUNK]).astype(BF16) for i in idx]
            st = [st[i] * gl[i][cs[i]:cs[i] + 1, :]
                  + jnp.dot(kdt_s[slot, items[i][0], items[i][1], :, rs[i]], v_new[i], preferred_element_type=F32)
                  for i in idx]
            if mode is not None:
                for i in idx:
                    hh, d, sc = items[i]
                    o = ws[i][CHUNK:] + jnp.dot(at_s[slot, hh, d, rs[i], :], v_new[i], preferred_element_type=F32)
                    ro = pl.multiple_of(sc * SUPER - n_ctx + cs[i] * CHUNK, CHUNK)
                    if mode == "set":
                        o_ref[0, pl.ds(ro, CHUNK), hh * DH:(hh + 1) * DH] = o
                    else:
                        o_ref[0, pl.ds(ro, CHUNK), hh * DH:(hh + 1) * DH] += o
            yield
        for i in idx:
            st_s[items[i][0], items[i][1]] = st[i]

    def interleave(*gens):
        live = list(gens)
        while live:
            for gen in list(live):
                try:
                    next(gen)
                except StopIteration:
                    live.remove(gen)

    def lat_ab(sc):
        return abl_ref[0, pl.ds(pl.multiple_of((sc - 1) * SUPER, SUPER), SUPER), :]

    def step(t, mode, with_prep):
        slot = t % 2
        bwd_sc = (ns_lat + 1 - t) if mode is not None else 0
        gens = [scan([(hh, d, t if d == 0 else bwd_sc) for hh in range(HP) for d in range(2)], slot, mode)]
        if with_prep:
            ab_f, ab_b = lat_ab(t + 1), lat_ab(ns_lat - t)
            gens.append(prep([(hh, d, ab_f if d == 0 else ab_b, (t + 1) if d == 0 else (ns_lat - t))
                              for hh in range(HP) for d in range(2)], 1 - slot))
        interleave(*gens)

    st_s[...] = jnp.zeros_like(st_s)
    ab_c = abc_ref[0]
    interleave(prep([(hh, d, ab_c, 0) for hh in range(HP) for d in range(2)], 0))
    step(0, None, True)
    half = ns_lat // 2

    def run(lo, hi, mode):
        def body(t, _):
            step(t, mode, True)
            return 0
        lax.fori_loop(lo, hi, body, 0)

    run(1, half + 1, "set")
    run(half + 1, ns_lat, "add")
    step(ns_lat, "add", False)


def _gdn(aqkv_lat, aqkv_ctx, ab_lat, ab_ctx, conv_w):
    bsz, n_lat, _ = aqkv_lat.shape
    n_ctx = aqkv_ctx.shape[1]
    n_all = n_ctx + n_lat
    width = HP * DH
    groups = HEADS // HP
    cps = SUPER // CHUNK

    def stream(n, j0):
        return pl.BlockSpec((1, n, width), lambda b, p: (b, 0, j0 + p))

    def wspec(j0):
        return pl.BlockSpec((CONV_K, width), lambda b, p: (0, j0 + p))

    return pl.pallas_call(
        functools.partial(_gdn_kernel, n_ctx=n_ctx, n_lat=n_lat),
        grid=(bsz, groups),
        in_specs=[stream(n_lat, 0), stream(n_lat, groups), stream(n_lat, 2 * groups),
                  stream(n_ctx, 0), stream(n_ctx, groups), stream(n_ctx, 2 * groups),
                  pl.BlockSpec((1, n_lat, 128), lambda b, p: (b, 0, 0)),
                  pl.BlockSpec((1, n_ctx, 128), lambda b, p: (b, 0, 0)),
                  wspec(0), wspec(groups), wspec(2 * groups)],
        out_specs=pl.BlockSpec((1, n_lat, width), lambda b, p: (b, 0, p)),
        out_shape=jax.ShapeDtypeStruct((bsz, n_lat, QK), F32),
        scratch_shapes=[pltpu.VMEM((n_lat + 16, width), F32),
                        pltpu.VMEM((n_all, width), F32), pltpu.VMEM((n_all, width), F32),
                        pltpu.VMEM((n_all, width), F32),
                        pltpu.VMEM((2, HP, 2, cps, 2 * CHUNK, DH), BF16),
                        pltpu.VMEM((2, HP, 2, SUPER, DH), F32),
                        pltpu.VMEM((2, HP, 2, DH, SUPER), BF16),
                        pltpu.VMEM((2, HP, 2, SUPER, CHUNK), BF16),
                        pltpu.VMEM((2, HP, 2, 8, DH), F32),
                        pltpu.VMEM((HP, 2, DH, DH), F32)],
        compiler_params=pltpu.CompilerParams(vmem_limit_bytes=VMEM_LIMIT),
        name="gdn",
    )(aqkv_lat, aqkv_lat, aqkv_lat, aqkv_ctx, aqkv_ctx, aqkv_ctx, ab_lat, ab_ctx,
      conv_w, conv_w, conv_w)


def _gla_kernel(ql_ref, gfl_ref, gbl_ref, kfl_ref, kbl_ref, vl_ref,
                qc_ref, gfc_ref, gbc_ref, kfc_ref, kbc_ref, vc_ref,
                o_ref, st_s, *, n_ctx, n_lat):
    assert n_ctx == SUPER and n_lat % (2 * SUPER) == 0
    ns_lat = n_lat // SUPER
    cps = SUPER // CHUNK

    def scan_supers(items, mode):
        idx = range(len(items))
        revs = [it[2] == 1 for it in items]
        masks = {d: _scan_masks(d == 1) for d in {it[2] for it in items}}
        r0 = [pl.multiple_of(it[3] * SUPER, SUPER) for it in items]
        hs = [slice(it[1] * DH, (it[1] + 1) * DH) for it in items]
        st = [st_s[it[1], it[2]] for it in items]
        g = [items[i][0][1][0, pl.ds(r0[i], SUPER), hs[i]] for i in idx]
        k = [items[i][0][2][0, pl.ds(r0[i], SUPER), hs[i]].astype(F32) for i in idx]
        v = [items[i][0][3][0, pl.ds(r0[i], SUPER), hs[i]] for i in idx]
        gcum = [_dot_split3(jnp.where(masks[items[i][2]][3], 1.0, 0.0), g[i]) for i in idx]
        g_mid = [_chunk_rows(gcum[i], SUB // 2 if revs[i] else SUB // 2 - 1, SUB) for i in idx]
        g_tot = [_chunk_rows(gcum[i], 0 if revs[i] else CHUNK - 1, CHUNK) for i in idx]
        kd = [k[i] * jnp.exp(g_mid[i] - gcum[i]) for i in idx]
        k_out = [kd[i] * jnp.exp(g_tot[i] - g_mid[i]) for i in idx]
        if mode is not None:
            q = [items[i][0][0][0, pl.ds(r0[i], SUPER), hs[i]].astype(F32) for i in idx]
            row = lax.broadcasted_iota(jnp.int32, (SUPER, DH), 0) & (CHUNK - 1)
            first = [(row >= SUB) if revs[i] else (row < SUB) for i in idx]
            g_bnd = [_chunk_rows(gcum[i], SUB if revs[i] else SUB - 1, CHUNK) for i in idx]
            qd = [q[i] * jnp.exp(gcum[i] - g_mid[i]) for i in idx]
            e_b = [jnp.exp(jnp.where(first[i], g_bnd[i] - g_mid[i], g_mid[i] - g_bnd[i])) for i in idx]
            qo = [jnp.where(first[i], 0.0, qd[i] * e_b[i]) for i in idx]
            ko = [jnp.where(first[i], kd[i] * e_b[i], 0.0) for i in idx]
            sc1 = [_dot_nt(qd[i], kd[i]) for i in idx]
            sc2 = [_dot_nt(qo[i], ko[i]) for i in idx]
            attn = []
            for i in idx:
                ii, jj, same, incl, _ = masks[items[i][2]]
                attn.append(jnp.where(incl & ((ii ^ jj) < SUB), sc1[i], 0.0) + jnp.where(same, sc2[i], 0.0))
            o_intra = [_dot(attn[i], v[i]) for i in idx]
            q_in = [qd[i] * jnp.exp(g_mid[i]) for i in idx]
        outs = [[None] * cps for _ in idx]
        for step_c in range(cps):
            cs = [(cps - 1 - step_c) if revs[i] else step_c for i in idx]
            rs = [slice(c * CHUNK, (c + 1) * CHUNK) for c in cs]
            if mode is not None:
                for i in idx:
                    outs[i][cs[i]] = o_intra[i][rs[i]] + _dot_nt(q_in[i][rs[i]], st[i])
            st = [st[i] * jnp.exp(g_tot[i][cs[i] * CHUNK:cs[i] * CHUNK + 1, :])
                  + _dot(jnp.transpose(v[i][rs[i]].astype(F32)), k_out[i][rs[i]]) for i in idx]
        for i in idx:
            _, hh, d, _ = items[i]
            st_s[hh, d] = st[i]
            if mode == "set":
                o_ref[0, pl.ds(r0[i], SUPER), hs[i]] = jnp.concatenate(outs[i], axis=0)
            elif mode == "add":
                o_ref[0, pl.ds(r0[i], SUPER), hs[i]] += jnp.concatenate(outs[i], axis=0)

    lat = ((ql_ref, gfl_ref, kfl_ref, vl_ref), (ql_ref, gbl_ref, kbl_ref, vl_ref))
    ctx = ((qc_ref, gfc_ref, kfc_ref, vc_ref), (qc_ref, gbc_ref, kbc_ref, vc_ref))
    st_s[...] = jnp.zeros_like(st_s)
    scan_supers([(ctx[d], hh, d, 0) for hh in range(HP) for d in range(2)], None)

    def run(lo, hi, mode):
        def body(n, _):
            scan_supers([(lat[d], hh, d, n if d == 0 else ns_lat - 1 - n) for hh in range(HP) for d in range(2)],
                        mode)
            return 0
        lax.fori_loop(lo, hi, body, 0)

    run(0, ns_lat // 2, "set")
    run(ns_lat // 2, ns_lat, "add")


def _gla(lat, ctx):
    bsz, n_lat, _ = lat[0].shape
    n_ctx = ctx[0].shape[1]
    width = HP * DH
    groups = HEADS // HP

    def specs(n):
        one = lambda j0: pl.BlockSpec((1, n, width), lambda b, p: (b, 0, j0 + p))
        return [one(0), one(0), one(groups), one(0), one(groups), one(0)]

    def args(t):
        q, g, k, v = t
        return [q, g, g, k, k, v]

    return pl.pallas_call(
        functools.partial(_gla_kernel, n_ctx=n_ctx, n_lat=n_lat),
        grid=(bsz, groups),
        in_specs=specs(n_lat) + specs(n_ctx),
        out_specs=pl.BlockSpec((1, n_lat, width), lambda b, p: (b, 0, p)),
        out_shape=jax.ShapeDtypeStruct((bsz, n_lat, QK), F32),
        scratch_shapes=[pltpu.VMEM((HP, 2, DH, DH), F32)],
        compiler_params=pltpu.CompilerParams(vmem_limit_bytes=VMEM_LIMIT),
        name="gla",
    )(*args(lat), *args(ctx))


def _gated_rms(o, gain, gate):
    parts = []
    for hh in range(HEADS):
        oh = o[:, hh * DH:(hh + 1) * DH]
        ms = jnp.mean(oh * oh, axis=-1, keepdims=True)
        parts.append(oh * lax.rsqrt(ms + RMS_EPS))
    return jnp.concatenate(parts, axis=1) * gain * _silu(gate)


def _post_kernel(x_ref, oa_ref, ob_ref, shift_ref, scale_ref, gate_ref, w_ref, wa_ref, wb_ref, wo_ref,
                 ga_ref, gb_ref, lng_ref, lnb_ref, out_ref, ob_s):
    xt = _load_raster(x_ref, True)
    u = (_layer_norm(xt) * (1.0 + scale_ref[0]) + shift_ref[0]).astype(BF16)
    oa = _load_raster(oa_ref, True)
    for cl in range(COLS_PER_TILE):
        ob_s[:, cl, :] = ob_ref[0, cl]
    ob = ob_s[...].reshape(TOK_TILE, QK)

    za = jnp.dot(u, w_ref[:, 0:QK], preferred_element_type=F32)
    y_a = _dot(_gated_rms(oa, ga_ref[...], za), wa_ref[...])
    zb = jnp.dot(u, w_ref[:, QK:2 * QK], preferred_element_type=F32)
    y_b = _dot(_gated_rms(ob, gb_ref[...], zb), wb_ref[...])
    m_a = _sigmoid(jnp.dot(u, w_ref[:, 2 * QK:2 * QK + D_MODEL], preferred_element_type=F32))
    m_b = _sigmoid(jnp.dot(u, w_ref[:, 2 * QK + D_MODEL:], preferred_element_type=F32))
    sub = _dot(m_a * y_a + m_b * y_b, wo_ref[...])
    hres = DEEPNORM_ALPHA * xt + gate_ref[0] * sub
    y = _layer_norm(hres) * lng_ref[...] + lnb_ref[...]
    _store_raster(out_ref, y, True)


def _post_project(x, oa, ob_cm, shift, scale, gate, w_post, w_a_out, w_b_out, w_out, a_gain, b_gain, ln_g, ln_b):
    bsz, length, _ = x.shape
    rows = length // GRID_W
    nj = GRID_W // COLS_PER_TILE
    const = lambda shape: pl.BlockSpec(shape, lambda b, j: tuple(0 for _ in shape))
    rast = lambda width: pl.BlockSpec((1, rows, COLS_PER_TILE, width), lambda b, j: (b, 0, j, 0))
    modv = pl.BlockSpec((1, 1, D_MODEL), lambda b, j: (b, 0, 0))
    out = pl.pallas_call(
        _post_kernel,
        grid=(bsz, nj),
        in_specs=[rast(D_MODEL), rast(QK),
                  pl.BlockSpec((1, COLS_PER_TILE, rows, QK), lambda b, j: (b, j, 0, 0)),
                  modv, modv, modv,
                  const((D_MODEL, N_POST)), const((QK, D_MODEL)), const((QK, D_MODEL)),
                  const((D_MODEL, D_MODEL)), const((1, QK)), const((1, QK)),
                  const((1, D_MODEL)), const((1, D_MODEL))],
        out_specs=rast(D_MODEL),
        out_shape=jax.ShapeDtypeStruct((bsz, rows, GRID_W, D_MODEL), F32),
        scratch_shapes=[pltpu.VMEM((rows, COLS_PER_TILE, QK), F32)],
        compiler_params=pltpu.CompilerParams(vmem_limit_bytes=VMEM_LIMIT),
        name="post",
    )(x.reshape(bsz, rows, GRID_W, D_MODEL), oa.reshape(bsz, rows, GRID_W, QK),
      ob_cm.reshape(bsz, GRID_W, rows, QK), shift, scale, gate,
      w_post, w_a_out, w_b_out, w_out, a_gain, b_gain, ln_g, ln_b)
    return out.reshape(bsz, length, D_MODEL)


def kernel(x, c, ctx, c_ctx, w_mod, b_mod, w_in, conv_w, a_log, dt_bias, lb_param, a_norm_g, b_norm_g,
           w_a_out, w_b_out, w_out, ln_g, ln_b):
    assert w_mod.shape[0] == DEPTH
    bsz, length, _ = x.shape
    n_ctx = ctx.shape[1]
    f32 = lambda t: t.astype(F32)

    w = f32(w_in[0])
    o_alpha = 3 * QK
    o_agate = o_alpha + 4 * HEADS
    o_bq = o_agate + QK
    o_bgate = o_bq + 4 * QK
    w_pre = jnp.concatenate([w[:, :o_alpha], jnp.pad(w[:, o_alpha:o_agate], ((0, 0), (0, 128 - 4 * HEADS))),
                             w[:, o_bq:o_bgate]], axis=1).astype(BF16)
    w_post = jnp.concatenate([w[:, o_agate:o_bq], w[:, o_bgate:]], axis=1).astype(BF16)
    alog_vec = jnp.pad(f32(a_log[0]).reshape(1, 2 * HEADS), ((0, 0), (0, 128 - 2 * HEADS)))
    dtb_vec = jnp.pad(f32(dt_bias[0]).reshape(1, 2 * HEADS), ((0, 0), (0, 128 - 2 * HEADS)))
    lbp = f32(lb_param).reshape(DEPTH + 1, 2 * QK)

    mod = _modulation(f32(c), f32(c_ctx), f32(w_mod[0]), f32(b_mod[0]))
    shift_l = mod[:bsz, None, 0:D_MODEL]
    scale_l = mod[:bsz, None, D_MODEL:2 * D_MODEL]
    gate_l = mod[:bsz, None, 2 * D_MODEL:]
    shift_c = mod[bsz:bsz + 1, None, 0:D_MODEL]
    scale_c = mod[bsz:bsz + 1, None, D_MODEL:2 * D_MODEL]

    lat = _pre_project(f32(x), shift_l, scale_l, w_pre, alog_vec, dtb_vec, lbp, cm=True)
    cx = _pre_project(f32(ctx).reshape(bsz * n_ctx, D_MODEL), shift_c, scale_c, w_pre, alog_vec, dtb_vec, lbp,
                      cm=False)
    aqkv_l = lat[0].reshape(bsz, length, 3 * QK)
    ab_l = lat[1].reshape(bsz, length, 128)
    gla_l = tuple(t.reshape(bsz, length, t.shape[-1]) for t in lat[2:])
    aqkv_c = cx[0].reshape(bsz, n_ctx, 3 * QK)
    ab_c = cx[1].reshape(bsz, n_ctx, 128)
    gla_c = tuple(t.reshape(bsz, n_ctx, t.shape[-1]) for t in cx[2:])

    oa = _gdn(aqkv_l, aqkv_c, ab_l, ab_c, f32(conv_w[0]))
    ob_cm = _gla(gla_l, gla_c)

    a_gain = jnp.tile(f32(a_norm_g[0]), HEADS).reshape(1, QK)
    b_gain = jnp.tile(f32(b_norm_g[0]), HEADS).reshape(1, QK)
    out = _post_project(f32(x), oa, ob_cm, shift_l, scale_l, gate_l, w_post,
                        f32(w_a_out[0]).astype(BF16), f32(w_b_out[0]).astype(BF16), f32(w_out[0]).astype(BF16),
                        a_gain, b_gain, f32(ln_g[0]).reshape(1, D_MODEL), f32(ln_b[0]).reshape(1, D_MODEL))
    return out.astype(x.dtype)
```

```python
import functools

import jax
import jax.numpy as jnp
from jax import lax
from jax.experimental import pallas as pl
from jax.experimental.pallas import tpu as pltpu

D_MODEL = 1024
GRID_W = 64
HEADS = 4
DH = 128
QK = HEADS * DH
CONV_K = 5
CHUNK = 64
SUPER = 256
HALF = 128
SUB = 32
HP = 2
DEPTH = 1
DEEPNORM_ALPHA = (2 * DEPTH) ** 0.25
LN_EPS = 1e-6
RMS_EPS = 1e-6
L2_EPS = 1e-6

TOK_TILE = 512
COLS_PER_TILE = TOK_TILE // GRID_W
N_PRE = 3 * QK + 128 + 4 * QK
N_POST = 2 * QK + 2 * D_MODEL
VMEM_LIMIT = 56 * 1024 * 1024

F32 = jnp.float32
BF16 = jnp.bfloat16


def _dot(a, b):
    return jnp.dot(a.astype(BF16), b.astype(BF16), preferred_element_type=F32)


def _dot_nt(a, b):
    return lax.dot_general(a.astype(BF16), b.astype(BF16), (((1,), (1,)), ((), ())),
                           preferred_element_type=F32)


def _dot_exact(a, b):
    return jnp.dot(a, b, preferred_element_type=F32, precision=lax.Precision.HIGHEST)


def _sigmoid(x):
    return 1.0 / (1.0 + jnp.exp(-x))


def _sigmoid_pair(x):
    t = jnp.exp(-jnp.abs(x))
    r = 1.0 / (1.0 + t)
    tr = t * r
    pos = x >= 0
    return jnp.where(pos, r, tr), jnp.where(pos, tr, r)


def _silu(x):
    return x * _sigmoid(x)


def _layer_norm(t):
    mu = jnp.mean(t, axis=-1, keepdims=True)
    tc = t - mu
    var = jnp.mean(tc * tc, axis=-1, keepdims=True)
    return tc * lax.rsqrt(var + LN_EPS)


def _mod_kernel(c_ref, w_ref, b_ref, o_ref):
    o_ref[...] = _dot_exact(_silu(c_ref[...]), w_ref[...]) + b_ref[...]


def _modulation(c, c_ctx, w_mod, b_mod):
    bsz = c.shape[0]
    rows = 16
    cc = jnp.zeros((rows, D_MODEL), F32).at[:bsz].set(c).at[bsz].set(c_ctx)
    nblk = 3
    out = pl.pallas_call(
        _mod_kernel,
        grid=(nblk,),
        in_specs=[pl.BlockSpec((rows, D_MODEL), lambda j: (0, 0)),
                  pl.BlockSpec((D_MODEL, D_MODEL), lambda j: (0, j)),
                  pl.BlockSpec((1, D_MODEL), lambda j: (0, j))],
        out_specs=pl.BlockSpec((rows, D_MODEL), lambda j: (0, j)),
        out_shape=jax.ShapeDtypeStruct((rows, 3 * D_MODEL), F32),
        name="mod",
    )(cc, w_mod, b_mod.reshape(1, 3 * D_MODEL))
    return out


def _load_raster(ref, grid_tile):
    return ref[0].reshape(TOK_TILE, ref.shape[-1]) if grid_tile else ref[...]


def _load_colmajor(ref):
    return jnp.concatenate([ref[0, :, cl, :] for cl in range(COLS_PER_TILE)], axis=0)


def _store_raster(ref, val, grid_tile, lanes=slice(None)):
    if grid_tile:
        ref[0, :, :, lanes] = val.reshape(GRID_W, COLS_PER_TILE, val.shape[-1]).astype(ref.dtype)
    else:
        ref[:, lanes] = val.astype(ref.dtype)


def _store_colmajor(ref, val, grid_tile, lanes=slice(None)):
    if grid_tile:
        for cl in range(COLS_PER_TILE):
            ref[0, cl, :, lanes] = val[cl * GRID_W:(cl + 1) * GRID_W, :].astype(ref.dtype)
    else:
        ref[:, lanes] = val.astype(ref.dtype)


def _pre_kernel(x_ref, shift_ref, scale_ref, w_ref, alog_ref, dtb_ref, lbp_ref,
                aqkv_ref, ab_ref, bq_ref, bg_ref, bk_ref, bi_ref, *, cm):
    modulate = lambda t: (_layer_norm(t) * (1.0 + scale_ref[0]) + shift_ref[0]).astype(BF16)
    u = modulate(_load_raster(x_ref, cm))
    u_b = modulate(_load_colmajor(x_ref)) if cm else u

    for g in range(3):
        z = jnp.dot(u, w_ref[:, g * QK:(g + 1) * QK], preferred_element_type=F32)
        _store_raster(aqkv_ref, z, cm, slice(g * QK, (g + 1) * QK))

    off = 3 * QK
    z = jnp.dot(u, w_ref[:, off:off + 128], preferred_element_type=F32)
    zs = z + dtb_ref[...]
    softplus = jnp.maximum(zs, 0.0) + jnp.log(1.0 + jnp.exp(-jnp.abs(zs)))
    a_g = -jnp.exp(alog_ref[...]) * softplus
    a_b = _sigmoid(z)
    lane = lax.broadcasted_iota(jnp.int32, z.shape, 1)
    _store_raster(ab_ref, jnp.where(lane < 2 * HEADS, a_g, a_b), cm)

    off += 128
    z = jnp.dot(u_b, w_ref[:, off:off + QK], preferred_element_type=F32)
    _store_colmajor(bq_ref, _silu(z) * DH ** -0.5, cm)

    off += QK
    p0 = lbp_ref[0:1, :]
    p1 = lbp_ref[1:2, :]
    pm = jnp.maximum(p0, p1)
    e0 = jnp.exp(p0 - pm)
    lb = e0 / (e0 + jnp.exp(p1 - pm))
    for d in range(2):
        z = jnp.dot(u_b, w_ref[:, off + d * QK: off + (d + 1) * QK], preferred_element_type=F32)
        s_pos, s_neg = _sigmoid_pair(z)
        lbd = lb[:, d * QK:(d + 1) * QK]
        _store_colmajor(bg_ref, jnp.log(lbd + (1.0 - lbd) * s_pos), cm, slice(d * QK, (d + 1) * QK))
        _store_colmajor(bk_ref, (1.0 - lbd) * s_neg, cm, slice(d * QK, (d + 1) * QK))

    off += 2 * QK
    z = jnp.dot(u_b, w_ref[:, off:off + QK], preferred_element_type=F32)
    _store_colmajor(bi_ref, z, cm)


def _pre_project(tokens, shift, scale, w_pre, alog_vec, dtb_vec, lbp, *, cm):
    consts = [pl.BlockSpec((D_MODEL, N_PRE), lambda *_: (0, 0)),
              pl.BlockSpec((1, 128), lambda *_: (0, 0)),
              pl.BlockSpec((1, 128), lambda *_: (0, 0)),
              pl.BlockSpec((2, 2 * QK), lambda *_: (0, 0))]
    if cm:
        bsz, length, _ = tokens.shape
        rows = length // GRID_W
        assert rows == GRID_W and GRID_W % COLS_PER_TILE == 0
        nj = GRID_W // COLS_PER_TILE
        x_view = tokens.reshape(bsz, rows, GRID_W, D_MODEL)
        grid = (bsz, nj)
        in_specs = [pl.BlockSpec((1, rows, COLS_PER_TILE, D_MODEL), lambda b, j: (b, 0, j, 0)),
                    pl.BlockSpec((1, 1, D_MODEL), lambda b, j: (b, 0, 0)),
                    pl.BlockSpec((1, 1, D_MODEL), lambda b, j: (b, 0, 0))] + consts

        def rast(width, dtype):
            return (jax.ShapeDtypeStruct((bsz, rows, GRID_W, width), dtype),
                    pl.BlockSpec((1, rows, COLS_PER_TILE, width), lambda b, j: (b, 0, j, 0)))

        def colm(width, dtype):
            return (jax.ShapeDtypeStruct((bsz, GRID_W, rows, width), dtype),
                    pl.BlockSpec((1, COLS_PER_TILE, rows, width), lambda b, j: (b, j, 0, 0)))
    else:
        n_tok = tokens.shape[0]
        assert n_tok % TOK_TILE == 0
        x_view = tokens
        grid = (n_tok // TOK_TILE,)
        in_specs = [pl.BlockSpec((TOK_TILE, D_MODEL), lambda i: (i, 0)),
                    pl.BlockSpec((1, 1, D_MODEL), lambda i: (0, 0, 0)),
                    pl.BlockSpec((1, 1, D_MODEL), lambda i: (0, 0, 0))] + consts

        def rast(width, dtype):
            return (jax.ShapeDtypeStruct((n_tok, width), dtype),
                    pl.BlockSpec((TOK_TILE, width), lambda i: (i, 0)))

        colm = rast

    outs = [rast(3 * QK, BF16), rast(128, F32), colm(QK, BF16), colm(2 * QK, F32),
            colm(2 * QK, BF16), colm(QK, BF16)]
    res = pl.pallas_call(
        functools.partial(_pre_kernel, cm=cm),
        grid=grid,
        in_specs=in_specs,
        out_specs=[o[1] for o in outs],
        out_shape=[o[0] for o in outs],
        compiler_params=pltpu.CompilerParams(vmem_limit_bytes=VMEM_LIMIT),
        name="pre_lat" if cm else "pre_ctx",
    )(x_view, shift, scale, w_pre, alog_vec, dtb_vec, lbp)
    return res


def _iota2(n):
    return (lax.broadcasted_iota(jnp.int32, (n, n), 0), lax.broadcasted_iota(jnp.int32, (n, n), 1))


def _scan_masks(rev):
    ii, jj = _iota2(SUPER)
    same = (ii ^ jj) < CHUNK
    incl = same & ((jj >= ii) if rev else (jj <= ii))
    strict = same & ((jj > ii) if rev else (jj < ii))
    return ii, jj, same, incl, strict


def _dot_split2(m01, g):
    m = m01.astype(BF16)
    g1 = g.astype(BF16)
    g2 = (g - g1.astype(F32)).astype(BF16)
    return jnp.dot(m, g1, preferred_element_type=F32) + jnp.dot(m, g2, preferred_element_type=F32)


def _chunk_rows(t, offset, span):
    return jnp.concatenate([jnp.broadcast_to(t[span * i + offset: span * i + offset + 1, :], (span, t.shape[1]))
                            for i in range(SUPER // span)], axis=0)


def _gdn_kernel(ql_ref, kl_ref, vl_ref, qc_ref, kc_ref, vc_ref, abl_ref, abc_ref,
                wq_ref, wk_ref, wv_ref, o_ref,
                q_s, k_s, v_s, wq_s, u_s, kdt_s, at_s, gl_s, st_s, *, n_ctx, n_lat):
    head0 = pl.program_id(1) * HP
    assert n_ctx == SUPER and n_lat % (2 * SUPER) == 0
    ns_lat = n_lat // SUPER
    cps = SUPER // CHUNK
    width = HP * DH

    halo = 16

    def conv_block(kind, src_ref, w_ref, dst_ref, src_r0, dst_r0, left_edge, right_edge):
        lo = 0 if left_edge else halo
        hi = 0 if right_edge else halo
        start = src_r0 - lo if isinstance(src_r0, int) else pl.multiple_of(src_r0 - lo, halo)
        parts = [src_ref[0, pl.ds(start, SUPER + lo + hi), :].astype(F32)]
        if left_edge:
            parts.insert(0, jnp.zeros((halo, width), F32))
        if right_edge:
            parts.append(jnp.zeros((halo, width), F32))
        xv = jnp.concatenate(parts, axis=0) if len(parts) > 1 else parts[0]
        acc = jnp.zeros((SUPER, width), F32)
        for j in range(CONV_K):
            sh = (CONV_K // 2 - j) % (SUPER + 2 * halo)
            tap = xv if sh == 0 else pltpu.roll(xv, sh, 0)
            acc = acc + tap[halo:halo + SUPER, :] * w_ref[j:j + 1, :]
        y = _silu(acc)
        for hh in range(HP):
            yh = y[:, hh * DH:(hh + 1) * DH]
            if kind != "v":
                yh = yh * lax.rsqrt(jnp.sum(yh * yh, axis=-1, keepdims=True) + L2_EPS)
            if kind == "q":
                yh = yh * DH ** -0.5
            dst_ref[pl.ds(dst_r0, SUPER), hh * DH:(hh + 1) * DH] = yh

    streams = (("q", ql_ref, qc_ref, wq_ref, q_s), ("k", kl_ref, kc_ref, wk_ref, k_s),
               ("v", vl_ref, vc_ref, wv_ref, v_s))

    def conv(blocks):
        for sc, left_edge, right_edge in blocks:
            for kind, lat_ref, ctx_ref, w_ref, dst_ref in streams:
                if isinstance(sc, int) and sc == 0:
                    conv_block(kind, ctx_ref, w_ref, dst_ref, 0, 0, True, True)
                else:
                    conv_block(kind, lat_ref, w_ref, dst_ref, (sc - 1) * SUPER,
                               sc * SUPER if isinstance(sc, int) else pl.multiple_of(sc * SUPER, SUPER),
                               left_edge, right_edge)
                yield

    lane = lax.broadcasted_iota(jnp.int32, (SUPER, 128), 1)

    def prep(items, slot):
        idx = range(len(items))
        masks = {d: _scan_masks(d == 1) for d in {it[1] for it in items}}
        incl = [masks[it[1]][3] for it in items]
        strict = [masks[it[1]][4] for it in items]
        revs = [it[1] == 1 for it in items]
        r0 = [pl.multiple_of(it[3] * SUPER, SUPER) for it in items]
        hs = [slice(it[0] * DH, (it[0] + 1) * DH) for it in items]
        q = [q_s[pl.ds(r0[i], SUPER), hs[i]] for i in idx]
        k = [k_s[pl.ds(r0[i], SUPER), hs[i]] for i in idx]
        v = [v_s[pl.ds(r0[i], SUPER), hs[i]] for i in idx]
        kq = [_dot_nt(jnp.concatenate([k[i], q[i]], axis=0), k[i]) for i in idx]
        cols = [it[1] * HEADS + head0 + it[0] for it in items]
        pick = lambda ab, col: jnp.broadcast_to(
            jnp.sum(jnp.where(lane == col, ab, 0.0), axis=-1, keepdims=True), (SUPER, DH))
        beta = [pick(items[i][2], 2 * HEADS + cols[i]) for i in idx]
        lower = jnp.where(masks[0][3] if 0 in masks else _scan_masks(False)[3], 1.0, 0.0)
        prefix = {}
        for it in items:
            if id(it[2]) not in prefix:
                prefix[id(it[2])] = _dot_split2(lower, it[2])
        gcum = []
        for i in idx:
            pre = pick(prefix[id(items[i][2])], cols[i])
            if revs[i]:
                pre = _chunk_rows(pre, CHUNK - 1, CHUNK) - pre + pick(items[i][2], cols[i])
            gcum.append(pre)
        yield
        gtot = [_chunk_rows(gcum[i], 0 if revs[i] else CHUNK - 1, CHUNK) for i in idx]
        grow = [jnp.transpose(gcum[i])[0:1, :] for i in idx]
        decay = [jnp.where(incl[i], jnp.exp(jnp.where(
            incl[i], jnp.concatenate([gcum[i], gcum[i]], axis=1) - grow[i], 0.0)), 0.0) for i in idx]
        e_g = [jnp.exp(gcum[i]) for i in idx]
        a_mat = [jnp.where(strict[i], kq[i][:SUPER] * jnp.concatenate([beta[i], beta[i]], axis=1) * decay[i], 0.0)
                 for i in idx]
        attn = [kq[i][SUPER:] * decay[i] for i in idx]
        for c in range(cps):
            rs = slice(c * CHUNK, (c + 1) * CHUNK)
            for i in idx:
                at_s[slot, items[i][0], items[i][1], rs, :] = attn[i][rs, rs].astype(BF16)
        halves = [(i, h) for i in idx for h in range(SUPER // HALF)]
        a_half = [a_mat[i][h * HALF:(h + 1) * HALF, h * HALF:(h + 1) * HALF] for i, h in halves]
        ii, jj = _iota2(HALF)
        eye = jnp.where(ii == jj, 1.0, 0.0).astype(F32)
        t = None
        s = 1
        while s < CHUNK:
            a_off = []
            for n, (i, _) in enumerate(halves):
                row, col = (jj, ii) if revs[i] else (ii, jj)
                m = ((row ^ col) < 2 * s) & ((row & s) != 0) & ((col & s) == 0)
                a_off.append(jnp.where(m, a_half[n], 0.0))
            if t is None:
                t = [eye - a for a in a_off]
            else:
                p = [_dot(a_off[n], t[n]) for n in range(len(halves))]
                yield
                t = [t[n] - _dot(t[n], p[n]) for n in range(len(halves))]
                yield
            s *= 2
        rhs = [jnp.concatenate([k[i] * beta[i] * e_g[i], v[i] * beta[i]], axis=1) for i in idx]
        sol_half = [_dot(t[n], rhs[i][h * HALF:(h + 1) * HALF]) for n, (i, h) in enumerate(halves)]
        sol = [jnp.concatenate([sol_half[n] for n, (j, _) in enumerate(halves) if j == i], axis=0)
               for i in idx]
        qg = [q[i] * e_g[i] for i in idx]
        kdt = [jnp.transpose(k[i] * jnp.exp(gtot[i] - gcum[i])) for i in idx]
        yield
        for i in idx:
            hh, d = items[i][0], items[i][1]
            for c in range(cps):
                rs = slice(c * CHUNK, (c + 1) * CHUNK)
                wq_s[slot, hh, d, c] = jnp.concatenate([sol[i][rs, :DH], qg[i][rs, :]], axis=0).astype(BF16)
            u_s[slot, hh, d] = sol[i][:, DH:]
            kdt_s[slot, hh, d] = kdt[i].astype(BF16)
            gl_s[slot, hh, d] = jnp.concatenate(
                [jnp.exp(gtot[i][c * CHUNK:c * CHUNK + 1, :]) for c in range(cps)]
                + [jnp.zeros((8 - cps, DH), F32)], axis=0)

    def scan(items, slot, mode):
        idx = range(len(items))
        st = [st_s[it[0], it[1]] for it in items]
        gl = [gl_s[slot, it[0], it[1]] for it in items]
        for step_c in range(cps):
            cs = [(cps - 1 - step_c) if it[1] == 1 else step_c for it in items]
            rs = [slice(c * CHUNK, (c + 1) * CHUNK) for c in cs]
            ws = [jnp.dot(wq_s[slot, items[i][0], items[i][1], cs[i]], st[i].astype(BF16),
                          preferred_element_type=F32) for i in idx]
            yield
            v_new = [(u_s[slot, items[i][0], items[i][1], rs[i], :] - ws[i][:CHUNK]).astype(BF16) for i in idx]
            st = [st[i] * gl[i][cs[i]:cs[i] + 1, :]
                  + jnp.dot(kdt_s[slot, items[i][0], items[i][1], :, rs[i]], v_new[i], preferred_element_type=F32)
                  for i in idx]
            if mode is not None:
                for i in idx:
                    hh, d, sc = items[i]
                    o = ws[i][CHUNK:] + jnp.dot(at_s[slot, hh, d, rs[i], :], v_new[i], preferred_element_type=F32)
                    ro = pl.multiple_of(sc * SUPER - n_ctx + cs[i] * CHUNK, CHUNK)
                    if mode == "set":
                        o_ref[0, pl.ds(ro, CHUNK), hh * DH:(hh + 1) * DH] = o
                    else:
                        o_ref[0, pl.ds(ro, CHUNK), hh * DH:(hh + 1) * DH] += o
            yield
        for i in idx:
            st_s[items[i][0], items[i][1]] = st[i]

    def interleave(*gens):
        live = list(gens)
        while live:
            for gen in list(live):
                try:
                    next(gen)
                except StopIteration:
                    live.remove(gen)

    def lat_ab(sc):
        return abl_ref[0, pl.ds(pl.multiple_of((sc - 1) * SUPER, SUPER), SUPER), :]

    def step(t, mode, with_prep, with_conv):
        slot = t % 2
        bwd_sc = (ns_lat + 1 - t) if mode is not None else 0
        gens = [scan([(hh, d, t if d == 0 else bwd_sc) for hh in range(HP) for d in range(2)], slot, mode)]
        if with_prep:
            ab_f, ab_b = lat_ab(t + 1), lat_ab(ns_lat - t)
            gens.append(prep([(hh, d, ab_f if d == 0 else ab_b, (t + 1) if d == 0 else (ns_lat - t))
                              for hh in range(HP) for d in range(2)], 1 - slot))
        if with_conv:
            gens.append(conv([(t + 2, False, False), (ns_lat - t - 1, False, False)]))
        interleave(*gens)

    st_s[...] = jnp.zeros_like(st_s)
    interleave(conv([(0, True, True)]))
    ab_c = abc_ref[0]
    interleave(prep([(hh, d, ab_c, 0) for hh in range(HP) for d in range(2)], 0),
               conv([(1, True, False), (ns_lat, False, True)]))
    step(0, None, True, True)
    half = ns_lat // 2
    conv_steps = (ns_lat - 2) // 2

    def run(lo, hi, mode, with_conv):
        def body(t, _):
            step(t, mode, True, with_conv)
            return 0
        lax.fori_loop(lo, hi, body, 0)

    run(1, conv_steps, "set", True)
    run(conv_steps, half + 1, "set", False)
    run(half + 1, ns_lat, "add", False)
    step(ns_lat, "add", False, False)


def _gdn(aqkv_lat, aqkv_ctx, ab_lat, ab_ctx, conv_w):
    bsz, n_lat, _ = aqkv_lat.shape
    n_ctx = aqkv_ctx.shape[1]
    n_all = n_ctx + n_lat
    width = HP * DH
    groups = HEADS // HP
    cps = SUPER // CHUNK

    def stream(n, j0):
        return pl.BlockSpec((1, n, width), lambda b, p: (b, 0, j0 + p))

    def wspec(j0):
        return pl.BlockSpec((CONV_K, width), lambda b, p: (0, j0 + p))

    return pl.pallas_call(
        functools.partial(_gdn_kernel, n_ctx=n_ctx, n_lat=n_lat),
        grid=(bsz, groups),
        in_specs=[stream(n_lat, 0), stream(n_lat, groups), stream(n_lat, 2 * groups),
                  stream(n_ctx, 0), stream(n_ctx, groups), stream(n_ctx, 2 * groups),
                  pl.BlockSpec((1, n_lat, 128), lambda b, p: (b, 0, 0)),
                  pl.BlockSpec((1, n_ctx, 128), lambda b, p: (b, 0, 0)),
                  wspec(0), wspec(groups), wspec(2 * groups)],
        out_specs=pl.BlockSpec((1, n_lat, width), lambda b, p: (b, 0, p)),
        out_shape=jax.ShapeDtypeStruct((bsz, n_lat, QK), F32),
        scratch_shapes=[pltpu.VMEM((n_all, width), F32), pltpu.VMEM((n_all, width), F32),
                        pltpu.VMEM((n_all, width), F32),
                        pltpu.VMEM((2, HP, 2, cps, 2 * CHUNK, DH), BF16),
                        pltpu.VMEM((2, HP, 2, SUPER, DH), F32),
                        pltpu.VMEM((2, HP, 2, DH, SUPER), BF16),
                        pltpu.VMEM((2, HP, 2, SUPER, CHUNK), BF16),
                        pltpu.VMEM((2, HP, 2, 8, DH), F32),
                        pltpu.VMEM((HP, 2, DH, DH), F32)],
        compiler_params=pltpu.CompilerParams(vmem_limit_bytes=VMEM_LIMIT),
        name="gdn",
    )(aqkv_lat, aqkv_lat, aqkv_lat, aqkv_ctx, aqkv_ctx, aqkv_ctx, ab_lat, ab_ctx,
      conv_w, conv_w, conv_w)


def _gla_kernel(ql_ref, gfl_ref, gbl_ref, kfl_ref, kbl_ref, vl_ref,
                qc_ref, gfc_ref, gbc_ref, kfc_ref, kbc_ref, vc_ref,
                o_ref, st_s, *, n_ctx, n_lat):
    assert n_ctx == SUPER and n_lat % (2 * SUPER) == 0
    ns_lat = n_lat // SUPER
    cps = SUPER // CHUNK

    def scan_supers(items, mode):
        idx = range(len(items))
        revs = [it[2] == 1 for it in items]
        masks = {d: _scan_masks(d == 1) for d in {it[2] for it in items}}
        r0 = [pl.multiple_of(it[3] * SUPER, SUPER) for it in items]
        hs = [slice(it[1] * DH, (it[1] + 1) * DH) for it in items]
        g = [items[i][0][1][0, pl.ds(r0[i], SUPER), hs[i]] for i in idx]
        k = [items[i][0][2][0, pl.ds(r0[i], SUPER), hs[i]].astype(F32) for i in idx]
        v = [items[i][0][3][0, pl.ds(r0[i], SUPER), hs[i]] for i in idx]
        lower = jnp.where(_scan_masks(False)[3], 1.0, 0.0)
        pre = {hh: _dot_split2(lower, jnp.concatenate([g[i] for i in idx if items[i][1] == hh], axis=1))
               for hh in sorted({it[1] for it in items})}
        yield
        gcum = [None] * len(items)
        for hh, pr in pre.items():
            for n, i in enumerate([i for i in idx if items[i][1] == hh]):
                p = pr[:, n * DH:(n + 1) * DH]
                gcum[i] = (_chunk_rows(p, CHUNK - 1, CHUNK) - p + g[i]) if revs[i] else p
        g_mid = [_chunk_rows(gcum[i], SUB // 2 if revs[i] else SUB // 2 - 1, SUB) for i in idx]
        g_tot = [_chunk_rows(gcum[i], 0 if revs[i] else CHUNK - 1, CHUNK) for i in idx]
        kd = [k[i] * jnp.exp(g_mid[i] - gcum[i]) for i in idx]
        k_out = [kd[i] * jnp.exp(g_tot[i] - g_mid[i]) for i in idx]
        if mode is not None:
            q = [items[i][0][0][0, pl.ds(r0[i], SUPER), hs[i]].astype(F32) for i in idx]
            row = lax.broadcasted_iota(jnp.int32, (SUPER, DH), 0) & (CHUNK - 1)
            first = [(row >= SUB) if revs[i] else (row < SUB) for i in idx]
            g_bnd = [_chunk_rows(gcum[i], SUB if revs[i] else SUB - 1, CHUNK) for i in idx]
            qd = [q[i] * jnp.exp(gcum[i] - g_mid[i]) for i in idx]
            e_b = [jnp.exp(jnp.where(first[i], g_bnd[i] - g_mid[i], g_mid[i] - g_bnd[i])) for i in idx]
            qo = [jnp.where(first[i], 0.0, qd[i] * e_b[i]) for i in idx]
            ko = [jnp.where(first[i], kd[i] * e_b[i], 0.0) for i in idx]
            sc1 = [_dot_nt(qd[i], kd[i]) for i in idx]
            sc2 = [_dot_nt(qo[i], ko[i]) for i in idx]
            q_in = [qd[i] * jnp.exp(g_mid[i]) for i in idx]
            yield
            attn = []
            for i in idx:
                ii, jj, same, incl, _ = masks[items[i][2]]
                attn.append(jnp.where(incl & ((ii ^ jj) < SUB), sc1[i], 0.0) + jnp.where(same, sc2[i], 0.0))
            o_intra = [_dot(attn[i], v[i]) for i in idx]
        vt = [[jnp.transpose(v[i][c * CHUNK:(c + 1) * CHUNK].astype(F32)) for c in range(cps)] for i in idx]
        yield
        st = [st_s[it[1], it[2]] for it in items]
        outs = [[None] * cps for _ in idx]
        for step_c in range(cps):
            cs = [(cps - 1 - step_c) if revs[i] else step_c for i in idx]
            rs = [slice(c * CHUNK, (c + 1) * CHUNK) for c in cs]
            if mode is not None:
                for i in idx:
                    outs[i][cs[i]] = o_intra[i][rs[i]] + _dot_nt(q_in[i][rs[i]], st[i])
            st = [st[i] * jnp.exp(g_tot[i][cs[i] * CHUNK:cs[i] * CHUNK + 1, :])
                  + _dot(vt[i][cs[i]], k_out[i][rs[i]]) for i in idx]
        for i in idx:
            _, hh, d, _ = items[i]
            st_s[hh, d] = st[i]
            if mode == "set":
                o_ref[0, pl.ds(r0[i], SUPER), hs[i]] = jnp.concatenate(outs[i], axis=0)
            elif mode == "add":
                o_ref[0, pl.ds(r0[i], SUPER), hs[i]] += jnp.concatenate(outs[i], axis=0)

    def interleave(*gens):
        live = list(gens)
        while live:
            for gen in list(live):
                try:
                    next(gen)
                except StopIteration:
                    live.remove(gen)

    lat = ((ql_ref, gfl_ref, kfl_ref, vl_ref), (ql_ref, gbl_ref, kbl_ref, vl_ref))
    ctx = ((qc_ref, gfc_ref, kfc_ref, vc_ref), (qc_ref, gbc_ref, kbc_ref, vc_ref))
    st_s[...] = jnp.zeros_like(st_s)
    interleave(scan_supers([(ctx[d], hh, d, 0) for hh in range(HP) for d in range(2)], None))

    def run(lo, hi, mode):
        def body(n, _):
            interleave(*[scan_supers([(lat[d], hh, d, (2 * n + e) if d == 0 else (ns_lat - 1 - 2 * n - e))
                                      for hh in range(HP) for d in range(2)], mode) for e in range(2)])
            return 0
        lax.fori_loop(lo, hi, body, 0)

    assert ns_lat % 4 == 0
    run(0, ns_lat // 4, "set")
    run(ns_lat // 4, ns_lat // 2, "add")


def _gla(lat, ctx):
    bsz, n_lat, _ = lat[0].shape
    n_ctx = ctx[0].shape[1]
    width = HP * DH
    groups = HEADS // HP

    def specs(n):
        one = lambda j0: pl.BlockSpec((1, n, width), lambda b, p: (b, 0, j0 + p))
        return [one(0), one(0), one(groups), one(0), one(groups), one(0)]

    def args(t):
        q, g, k, v = t
        return [q, g, g, k, k, v]

    return pl.pallas_call(
        functools.partial(_gla_kernel, n_ctx=n_ctx, n_lat=n_lat),
        grid=(bsz, groups),
        in_specs=specs(n_lat) + specs(n_ctx),
        out_specs=pl.BlockSpec((1, n_lat, width), lambda b, p: (b, 0, p)),
        out_shape=jax.ShapeDtypeStruct((bsz, n_lat, QK), F32),
        scratch_shapes=[pltpu.VMEM((HP, 2, DH, DH), F32)],
        compiler_params=pltpu.CompilerParams(vmem_limit_bytes=VMEM_LIMIT),
        name="gla",
    )(*args(lat), *args(ctx))


def _gated_rms(o, gain, gate):
    parts = []
    for hh in range(HEADS):
        oh = o[:, hh * DH:(hh + 1) * DH]
        ms = jnp.mean(oh * oh, axis=-1, keepdims=True)
        parts.append(oh * lax.rsqrt(ms + RMS_EPS))
    return jnp.concatenate(parts, axis=1) * gain * _silu(gate)


def _post_kernel(x_ref, oa_ref, ob_ref, shift_ref, scale_ref, gate_ref, w_ref, wa_ref, wb_ref, wo_ref,
                 ga_ref, gb_ref, lng_ref, lnb_ref, out_ref, ob_s):
    xt = _load_raster(x_ref, True)
    u = (_layer_norm(xt) * (1.0 + scale_ref[0]) + shift_ref[0]).astype(BF16)
    oa = _load_raster(oa_ref, True)
    for cl in range(COLS_PER_TILE):
        ob_s[:, cl, :] = ob_ref[0, cl]
    ob = ob_s[...].reshape(TOK_TILE, QK)

    za = jnp.dot(u, w_ref[:, 0:QK], preferred_element_type=F32)
    y_a = _dot(_gated_rms(oa, ga_ref[...], za), wa_ref[...])
    zb = jnp.dot(u, w_ref[:, QK:2 * QK], preferred_element_type=F32)
    y_b = _dot(_gated_rms(ob, gb_ref[...], zb), wb_ref[...])
    m_a = _sigmoid(jnp.dot(u, w_ref[:, 2 * QK:2 * QK + D_MODEL], preferred_element_type=F32))
    m_b = _sigmoid(jnp.dot(u, w_ref[:, 2 * QK + D_MODEL:], preferred_element_type=F32))
    sub = _dot(m_a * y_a + m_b * y_b, wo_ref[...])
    hres = DEEPNORM_ALPHA * xt + gate_ref[0] * sub
    y = _layer_norm(hres) * lng_ref[...] + lnb_ref[...]
    _store_raster(out_ref, y, True)


def _post_project(x, oa, ob_cm, shift, scale, gate, w_post, w_a_out, w_b_out, w_out, a_gain, b_gain, ln_g, ln_b):
    bsz, length, _ = x.shape
    rows = length // GRID_W
    nj = GRID_W // COLS_PER_TILE
    const = lambda shape: pl.BlockSpec(shape, lambda b, j: tuple(0 for _ in shape))
    rast = lambda width: pl.BlockSpec((1, rows, COLS_PER_TILE, width), lambda b, j: (b, 0, j, 0))
    modv = pl.BlockSpec((1, 1, D_MODEL), lambda b, j: (b, 0, 0))
    out = pl.pallas_call(
        _post_kernel,
        grid=(bsz, nj),
        in_specs=[rast(D_MODEL), rast(QK),
                  pl.BlockSpec((1, COLS_PER_TILE, rows, QK), lambda b, j: (b, j, 0, 0)),
                  modv, modv, modv,
                  const((D_MODEL, N_POST)), const((QK, D_MODEL)), const((QK, D_MODEL)),
                  const((D_MODEL, D_MODEL)), const((1, QK)), const((1, QK)),
                  const((1, D_MODEL)), const((1, D_MODEL))],
        out_specs=rast(D_MODEL),
        out_shape=jax.ShapeDtypeStruct((bsz, rows, GRID_W, D_MODEL), F32),
        scratch_shapes=[pltpu.VMEM((rows, COLS_PER_TILE, QK), F32)],
        compiler_params=pltpu.CompilerParams(vmem_limit_bytes=VMEM_LIMIT),
        name="post",
    )(x.reshape(bsz, rows, GRID_W, D_MODEL), oa.reshape(bsz, rows, GRID_W, QK),
      ob_cm.reshape(bsz, GRID_W, rows, QK), shift, scale, gate,
      w_post, w_a_out, w_b_out, w_out, a_gain, b_gain, ln_g, ln_b)
    return out.reshape(bsz, length, D_MODEL)


def kernel(x, c, ctx, c_ctx, w_mod, b_mod, w_in, conv_w, a_log, dt_bias, lb_param, a_norm_g, b_norm_g,
           w_a_out, w_b_out, w_out, ln_g, ln_b):
    assert w_mod.shape[0] == DEPTH
    bsz, length, _ = x.shape
    n_ctx = ctx.shape[1]
    f32 = lambda t: t.astype(F32)

    w = f32(w_in[0])
    o_alpha = 3 * QK
    o_agate = o_alpha + 4 * HEADS
    o_bq = o_agate + QK
    o_bgate = o_bq + 4 * QK
    w_pre = jnp.concatenate([w[:, :o_alpha], jnp.pad(w[:, o_alpha:o_agate], ((0, 0), (0, 128 - 4 * HEADS))),
                             w[:, o_bq:o_bgate]], axis=1).astype(BF16)
    w_post = jnp.concatenate([w[:, o_agate:o_bq], w[:, o_bgate:]], axis=1).astype(BF16)
    alog_vec = jnp.pad(f32(a_log[0]).reshape(1, 2 * HEADS), ((0, 0), (0, 128 - 2 * HEADS)))
    dtb_vec = jnp.pad(f32(dt_bias[0]).reshape(1, 2 * HEADS), ((0, 0), (0, 128 - 2 * HEADS)))
    lbp = f32(lb_param).reshape(DEPTH + 1, 2 * QK)

    mod = _modulation(f32(c), f32(c_ctx), f32(w_mod[0]), f32(b_mod[0]))
    shift_l = mod[:bsz, None, 0:D_MODEL]
    scale_l = mod[:bsz, None, D_MODEL:2 * D_MODEL]
    gate_l = mod[:bsz, None, 2 * D_MODEL:]
    shift_c = mod[bsz:bsz + 1, None, 0:D_MODEL]
    scale_c = mod[bsz:bsz + 1, None, D_MODEL:2 * D_MODEL]

    lat = _pre_project(f32(x), shift_l, scale_l, w_pre, alog_vec, dtb_vec, lbp, cm=True)
    cx = _pre_project(f32(ctx).reshape(bsz * n_ctx, D_MODEL), shift_c, scale_c, w_pre, alog_vec, dtb_vec, lbp,
                      cm=False)
    aqkv_l = lat[0].reshape(bsz, length, 3 * QK)
    ab_l = lat[1].reshape(bsz, length, 128)
    gla_l = tuple(t.reshape(bsz, length, t.shape[-1]) for t in lat[2:])
    aqkv_c = cx[0].reshape(bsz, n_ctx, 3 * QK)
    ab_c = cx[1].reshape(bsz, n_ctx, 128)
    gla_c = tuple(t.reshape(bsz, n_ctx, t.shape[-1]) for t in cx[2:])

    oa = _gdn(aqkv_l, aqkv_c, ab_l, ab_c, f32(conv_w[0]))
    ob_cm = _gla(gla_l, gla_c)

    a_gain = jnp.tile(f32(a_norm_g[0]), HEADS).reshape(1, QK)
    b_gain = jnp.tile(f32(b_norm_g[0]), HEADS).reshape(1, QK)
    out = _post_project(f32(x), oa, ob_cm, shift_l, scale_l, gate_l, w_post,
                        f32(w_a_out[0]).astype(BF16), f32(w_b_out[0]).astype(BF16), f32(w_out[0]).astype(BF16),
                        a_gain, b_gain, f32(ln_g[0]).reshape(1, D_MODEL), f32(ln_b[0]).reshape(1, D_MODEL))
    return out.astype(x.dtype)
```

```python
import functools

import jax
import jax.numpy as jnp
from jax import lax
from jax.experimental import pallas as pl
from jax.experimental.pallas import tpu as pltpu

D_MODEL = 1024
GRID_W = 64
HEADS = 4
DH = 128
QK = HEADS * DH
CONV_K = 5
CHUNK = 64
SUPER = 256
HALF = 128
SUB = 32
HP = 2
DEPTH = 1
DEEPNORM_ALPHA = (2 * DEPTH) ** 0.25
LN_EPS = 1e-6
RMS_EPS = 1e-6
L2_EPS = 1e-6

TOK_TILE = 512
COLS_PER_TILE = TOK_TILE // GRID_W
PRE_PARTS = 4
POST_PARTS = 2
N_PRE = 3 * QK + 128 + 4 * QK
N_POST = 2 * QK + 2 * D_MODEL
VMEM_LIMIT = 56 * 1024 * 1024

F32 = jnp.float32
BF16 = jnp.bfloat16


def _dot(a, b):
    return jnp.dot(a.astype(BF16), b.astype(BF16), preferred_element_type=F32)


def _dot_nt(a, b):
    return lax.dot_general(a.astype(BF16), b.astype(BF16), (((1,), (1,)), ((), ())),
                           preferred_element_type=F32)


def _dot_exact(a, b):
    return jnp.dot(a, b, preferred_element_type=F32, precision=lax.Precision.HIGHEST)


def _sigmoid(x):
    return 1.0 / (1.0 + jnp.exp(-x))


def _sigmoid_pair(x):
    t = jnp.exp(-jnp.abs(x))
    r = 1.0 / (1.0 + t)
    tr = t * r
    pos = x >= 0
    return jnp.where(pos, r, tr), jnp.where(pos, tr, r)


def _silu(x):
    return x * _sigmoid(x)


def _layer_norm(t):
    mu = jnp.mean(t, axis=-1, keepdims=True)
    tc = t - mu
    var = jnp.mean(tc * tc, axis=-1, keepdims=True)
    return tc * lax.rsqrt(var + LN_EPS)


def _mod_kernel(c_ref, w_ref, b_ref, o_ref):
    o_ref[...] = _dot_exact(_silu(c_ref[...]), w_ref[...]) + b_ref[...]


def _modulation(c, c_ctx, w_mod, b_mod):
    bsz = c.shape[0]
    rows = 16
    cc = jnp.zeros((rows, D_MODEL), F32).at[:bsz].set(c).at[bsz].set(c_ctx)
    nblk = 3
    out = pl.pallas_call(
        _mod_kernel,
        grid=(nblk,),
        in_specs=[pl.BlockSpec((rows, D_MODEL), lambda j: (0, 0)),
                  pl.BlockSpec((D_MODEL, D_MODEL), lambda j: (0, j)),
                  pl.BlockSpec((1, D_MODEL), lambda j: (0, j))],
        out_specs=pl.BlockSpec((rows, D_MODEL), lambda j: (0, j)),
        out_shape=jax.ShapeDtypeStruct((rows, 3 * D_MODEL), F32),
        name="mod",
    )(cc, w_mod, b_mod.reshape(1, 3 * D_MODEL))
    return out


def _interleave_skewed(gens):
    live = list(enumerate(gens))
    rnd = 0
    while live:
        for i, gen in list(live):
            if rnd >= i:
                try:
                    next(gen)
                except StopIteration:
                    live.remove((i, gen))
        rnd += 1


def _load_raster(ref, grid_tile, part, parts):
    tok = TOK_TILE // parts
    if grid_tile:
        rows = GRID_W // parts
        return ref[0, part * rows:(part + 1) * rows].reshape(tok, ref.shape[-1])
    return ref[part * tok:(part + 1) * tok, :]


def _load_colmajor(ref, part, parts):
    cols = COLS_PER_TILE // parts
    return jnp.concatenate([ref[0, :, cl, :] for cl in range(part * cols, (part + 1) * cols)], axis=0)


def _store_raster(ref, val, grid_tile, part, parts, lanes=slice(None)):
    tok = TOK_TILE // parts
    if grid_tile:
        rows = GRID_W // parts
        ref[0, part * rows:(part + 1) * rows, :, lanes] = (
            val.reshape(rows, COLS_PER_TILE, val.shape[-1]).astype(ref.dtype))
    else:
        ref[part * tok:(part + 1) * tok, lanes] = val.astype(ref.dtype)


def _store_colmajor(ref, val, grid_tile, part, parts, lanes=slice(None)):
    tok = TOK_TILE // parts
    if grid_tile:
        cols = COLS_PER_TILE // parts
        for n, cl in enumerate(range(part * cols, (part + 1) * cols)):
            ref[0, cl, :, lanes] = val[n * GRID_W:(n + 1) * GRID_W, :].astype(ref.dtype)
    else:
        ref[part * tok:(part + 1) * tok, lanes] = val.astype(ref.dtype)


def _pre_kernel(x_ref, shift_ref, scale_ref, w_ref, alog_ref, dtb_ref, lbp_ref,
                aqkv_ref, ab_ref, bq_ref, bg_ref, bk_ref, bi_ref, *, cm):
    modulate = lambda t: (_layer_norm(t) * (1.0 + scale_ref[0]) + shift_ref[0]).astype(BF16)

    def part_stages(part):
        u = modulate(_load_raster(x_ref, cm, part, PRE_PARTS))
        yield
        for g in range(3):
            z = jnp.dot(u, w_ref[:, g * QK:(g + 1) * QK], preferred_element_type=F32)
            _store_raster(aqkv_ref, z, cm, part, PRE_PARTS, slice(g * QK, (g + 1) * QK))
            yield
        off = 3 * QK
        z = jnp.dot(u, w_ref[:, off:off + 128], preferred_element_type=F32)
        zs = z + dtb_ref[...]
        softplus = jnp.maximum(zs, 0.0) + jnp.log(1.0 + jnp.exp(-jnp.abs(zs)))
        a_g = -jnp.exp(alog_ref[...]) * softplus
        a_b = _sigmoid(z)
        lane = lax.broadcasted_iota(jnp.int32, z.shape, 1)
        _store_raster(ab_ref, jnp.where(lane < 2 * HEADS, a_g, a_b), cm, part, PRE_PARTS)
        u_b = modulate(_load_colmajor(x_ref, part, PRE_PARTS)) if cm else u
        yield
        off += 128
        z = jnp.dot(u_b, w_ref[:, off:off + QK], preferred_element_type=F32)
        _store_colmajor(bq_ref, _silu(z) * DH ** -0.5, cm, part, PRE_PARTS)
        yield
        off += QK
        p0 = lbp_ref[0:1, :]
        p1 = lbp_ref[1:2, :]
        pm = jnp.maximum(p0, p1)
        e0 = jnp.exp(p0 - pm)
        lb = e0 / (e0 + jnp.exp(p1 - pm))
        for d in range(2):
            z = jnp.dot(u_b, w_ref[:, off + d * QK: off + (d + 1) * QK], preferred_element_type=F32)
            s_pos, s_neg = _sigmoid_pair(z)
            lbd = lb[:, d * QK:(d + 1) * QK]
            lanes = slice(d * QK, (d + 1) * QK)
            _store_colmajor(bg_ref, jnp.log(lbd + (1.0 - lbd) * s_pos), cm, part, PRE_PARTS, lanes)
            _store_colmajor(bk_ref, (1.0 - lbd) * s_neg, cm, part, PRE_PARTS, lanes)
            yield
        off += 2 * QK
        z = jnp.dot(u_b, w_ref[:, off:off + QK], preferred_element_type=F32)
        _store_colmajor(bi_ref, z, cm, part, PRE_PARTS)

    _interleave_skewed([part_stages(p) for p in range(PRE_PARTS)])


def _pre_project(tokens, shift, scale, w_pre, alog_vec, dtb_vec, lbp, *, cm):
    consts = [pl.BlockSpec((D_MODEL, N_PRE), lambda *_: (0, 0)),
              pl.BlockSpec((1, 128), lambda *_: (0, 0)),
              pl.BlockSpec((1, 128), lambda *_: (0, 0)),
              pl.BlockSpec((2, 2 * QK), lambda *_: (0, 0))]
    if cm:
        bsz, length, _ = tokens.shape
        rows = length // GRID_W
        assert rows == GRID_W and GRID_W % COLS_PER_TILE == 0
        nj = GRID_W // COLS_PER_TILE
        x_view = tokens.reshape(bsz, rows, GRID_W, D_MODEL)
        grid = (bsz, nj)
        in_specs = [pl.BlockSpec((1, rows, COLS_PER_TILE, D_MODEL), lambda b, j: (b, 0, j, 0)),
                    pl.BlockSpec((1, 1, D_MODEL), lambda b, j: (b, 0, 0)),
                    pl.BlockSpec((1, 1, D_MODEL), lambda b, j: (b, 0, 0))] + consts

        def rast(width, dtype):
            return (jax.ShapeDtypeStruct((bsz, rows, GRID_W, width), dtype),
                    pl.BlockSpec((1, rows, COLS_PER_TILE, width), lambda b, j: (b, 0, j, 0)))

        def colm(width, dtype):
            return (jax.ShapeDtypeStruct((bsz, GRID_W, rows, width), dtype),
                    pl.BlockSpec((1, COLS_PER_TILE, rows, width), lambda b, j: (b, j, 0, 0)))
    else:
        n_tok = tokens.shape[0]
        assert n_tok % TOK_TILE == 0
        x_view = tokens
        grid = (n_tok // TOK_TILE,)
        in_specs = [pl.BlockSpec((TOK_TILE, D_MODEL), lambda i: (i, 0)),
                    pl.BlockSpec((1, 1, D_MODEL), lambda i: (0, 0, 0)),
                    pl.BlockSpec((1, 1, D_MODEL), lambda i: (0, 0, 0))] + consts

        def rast(width, dtype):
            return (jax.ShapeDtypeStruct((n_tok, width), dtype),
                    pl.BlockSpec((TOK_TILE, width), lambda i: (i, 0)))

        colm = rast

    outs = [rast(3 * QK, BF16), rast(128, F32), colm(QK, BF16), colm(2 * QK, F32),
            colm(2 * QK, BF16), colm(QK, BF16)]
    res = pl.pallas_call(
        functools.partial(_pre_kernel, cm=cm),
        grid=grid,
        in_specs=in_specs,
        out_specs=[o[1] for o in outs],
        out_shape=[o[0] for o in outs],
        compiler_params=pltpu.CompilerParams(vmem_limit_bytes=VMEM_LIMIT),
        name="pre_lat" if cm else "pre_ctx",
    )(x_view, shift, scale, w_pre, alog_vec, dtb_vec, lbp)
    return res


def _iota2(n):
    return (lax.broadcasted_iota(jnp.int32, (n, n), 0), lax.broadcasted_iota(jnp.int32, (n, n), 1))


def _scan_masks(rev):
    ii, jj = _iota2(SUPER)
    same = (ii ^ jj) < CHUNK
    incl = same & ((jj >= ii) if rev else (jj <= ii))
    strict = same & ((jj > ii) if rev else (jj < ii))
    return ii, jj, same, incl, strict


def _dot_split2(m01, g):
    m = m01.astype(BF16)
    g1 = g.astype(BF16)
    g2 = (g - g1.astype(F32)).astype(BF16)
    return jnp.dot(m, g1, preferred_element_type=F32) + jnp.dot(m, g2, preferred_element_type=F32)


def _chunk_rows(t, offset, span):
    return jnp.concatenate([jnp.broadcast_to(t[span * i + offset: span * i + offset + 1, :], (span, t.shape[1]))
                            for i in range(SUPER // span)], axis=0)


def _gdn_kernel(ql_ref, kl_ref, vl_ref, qc_ref, kc_ref, vc_ref, abl_ref, abc_ref,
                wq_ref, wk_ref, wv_ref, o_ref,
                q_s, k_s, v_s, wq_s, u_s, kdt_s, at_s, gl_s, st_s, *, n_ctx, n_lat):
    head0 = pl.program_id(1) * HP
    assert n_ctx == SUPER and n_lat % (2 * SUPER) == 0
    ns_lat = n_lat // SUPER
    cps = SUPER // CHUNK
    width = HP * DH

    halo = 16

    def conv_block(kind, src_ref, w_ref, dst_ref, src_r0, dst_r0, left_edge, right_edge):
        lo = 0 if left_edge else halo
        hi = 0 if right_edge else halo
        start = src_r0 - lo if isinstance(src_r0, int) else pl.multiple_of(src_r0 - lo, halo)
        parts = [src_ref[0, pl.ds(start, SUPER + lo + hi), :].astype(F32)]
        if left_edge:
            parts.insert(0, jnp.zeros((halo, width), F32))
        if right_edge:
            parts.append(jnp.zeros((halo, width), F32))
        xv = jnp.concatenate(parts, axis=0) if len(parts) > 1 else parts[0]
        acc = jnp.zeros((SUPER, width), F32)
        for j in range(CONV_K):
            sh = (CONV_K // 2 - j) % (SUPER + 2 * halo)
            tap = xv if sh == 0 else pltpu.roll(xv, sh, 0)
            acc = acc + tap[halo:halo + SUPER, :] * w_ref[j:j + 1, :]
        y = _silu(acc)
        for hh in range(HP):
            yh = y[:, hh * DH:(hh + 1) * DH]
            if kind != "v":
                yh = yh * lax.rsqrt(jnp.sum(yh * yh, axis=-1, keepdims=True) + L2_EPS)
            if kind == "q":
                yh = yh * DH ** -0.5
            dst_ref[pl.ds(dst_r0, SUPER), hh * DH:(hh + 1) * DH] = yh

    streams = (("q", ql_ref, qc_ref, wq_ref, q_s), ("k", kl_ref, kc_ref, wk_ref, k_s),
               ("v", vl_ref, vc_ref, wv_ref, v_s))

    def conv(blocks):
        for sc, left_edge, right_edge in blocks:
            for kind, lat_ref, ctx_ref, w_ref, dst_ref in streams:
                if isinstance(sc, int) and sc == 0:
                    conv_block(kind, ctx_ref, w_ref, dst_ref, 0, 0, True, True)
                else:
                    conv_block(kind, lat_ref, w_ref, dst_ref, (sc - 1) * SUPER,
                               sc * SUPER if isinstance(sc, int) else pl.multiple_of(sc * SUPER, SUPER),
                               left_edge, right_edge)
                yield

    lane = lax.broadcasted_iota(jnp.int32, (SUPER, 128), 1)

    def prep(items, slot):
        idx = range(len(items))
        masks = {d: _scan_masks(d == 1) for d in {it[1] for it in items}}
        incl = [masks[it[1]][3] for it in items]
        strict = [masks[it[1]][4] for it in items]
        revs = [it[1] == 1 for it in items]
        r0 = [pl.multiple_of(it[3] * SUPER, SUPER) for it in items]
        hs = [slice(it[0] * DH, (it[0] + 1) * DH) for it in items]
        q = [q_s[pl.ds(r0[i], SUPER), hs[i]] for i in idx]
        k = [k_s[pl.ds(r0[i], SUPER), hs[i]] for i in idx]
        v = [v_s[pl.ds(r0[i], SUPER), hs[i]] for i in idx]
        kq = [_dot_nt(jnp.concatenate([k[i], q[i]], axis=0), k[i]) for i in idx]
        cols = [it[1] * HEADS + head0 + it[0] for it in items]
        pick = lambda ab, col: jnp.broadcast_to(
            jnp.sum(jnp.where(lane == col, ab, 0.0), axis=-1, keepdims=True), (SUPER, DH))
        beta = [pick(items[i][2], 2 * HEADS + cols[i]) for i in idx]
        lower = jnp.where(masks[0][3] if 0 in masks else _scan_masks(False)[3], 1.0, 0.0)
        prefix = {}
        for it in items:
            if id(it[2]) not in prefix:
                prefix[id(it[2])] = _dot_split2(lower, it[2])
        gcum = []
        for i in idx:
            pre = pick(prefix[id(items[i][2])], cols[i])
            if revs[i]:
                pre = _chunk_rows(pre, CHUNK - 1, CHUNK) - pre + pick(items[i][2], cols[i])
            gcum.append(pre)
        yield
        gtot = [_chunk_rows(gcum[i], 0 if revs[i] else CHUNK - 1, CHUNK) for i in idx]
        grow = [jnp.transpose(gcum[i])[0:1, :] for i in idx]
        decay = [jnp.where(incl[i], jnp.exp(jnp.where(
            incl[i], jnp.concatenate([gcum[i], gcum[i]], axis=1) - grow[i], 0.0)), 0.0) for i in idx]
        e_g = [jnp.exp(gcum[i]) for i in idx]
        a_mat = [jnp.where(strict[i], kq[i][:SUPER] * jnp.concatenate([beta[i], beta[i]], axis=1) * decay[i], 0.0)
                 for i in idx]
        attn = [kq[i][SUPER:] * decay[i] for i in idx]
        for c in range(cps):
            rs = slice(c * CHUNK, (c + 1) * CHUNK)
            for i in idx:
                at_s[slot, items[i][0], items[i][1], rs, :] = attn[i][rs, rs].astype(BF16)
        halves = [(i, h) for i in idx for h in range(SUPER // HALF)]
        a_half = [a_mat[i][h * HALF:(h + 1) * HALF, h * HALF:(h + 1) * HALF] for i, h in halves]
        ii, jj = _iota2(HALF)
        eye = jnp.where(ii == jj, 1.0, 0.0).astype(F32)
        t = None
        s = 1
        while s < CHUNK:
            a_off = []
            for n, (i, _) in enumerate(halves):
                row, col = (jj, ii) if revs[i] else (ii, jj)
                m = ((row ^ col) < 2 * s) & ((row & s) != 0) & ((col & s) == 0)
                a_off.append(jnp.where(m, a_half[n], 0.0))
            if t is None:
                t = [eye - a for a in a_off]
            else:
                p = [_dot(a_off[n], t[n]) for n in range(len(halves))]
                yield
                t = [t[n] - _dot(t[n], p[n]) for n in range(len(halves))]
                yield
            s *= 2
        rhs = [jnp.concatenate([k[i] * beta[i] * e_g[i], v[i] * beta[i]], axis=1) for i in idx]
        sol_half = [_dot(t[n], rhs[i][h * HALF:(h + 1) * HALF]) for n, (i, h) in enumerate(halves)]
        sol = [jnp.concatenate([sol_half[n] for n, (j, _) in enumerate(halves) if j == i], axis=0)
               for i in idx]
        qg = [q[i] * e_g[i] for i in idx]
        kdt = [jnp.transpose(k[i] * jnp.exp(gtot[i] - gcum[i])) for i in idx]
        yield
        for i in idx:
            hh, d = items[i][0], items[i][1]
            for c in range(cps):
                rs = slice(c * CHUNK, (c + 1) * CHUNK)
                wq_s[slot, hh, d, c] = jnp.concatenate([sol[i][rs, :DH], qg[i][rs, :]], axis=0).astype(BF16)
            u_s[slot, hh, d] = sol[i][:, DH:]
            kdt_s[slot, hh, d] = kdt[i].astype(BF16)
            gl_s[slot, hh, d] = jnp.concatenate(
                [jnp.exp(gtot[i][c * CHUNK:c * CHUNK + 1, :]) for c in range(cps)]
                + [jnp.zeros((8 - cps, DH), F32)], axis=0)

    def scan(items, slot, mode):
        idx = range(len(items))
        st = [st_s[it[0], it[1]] for it in items]
        gl = [gl_s[slot, it[0], it[1]] for it in items]
        for step_c in range(cps):
            cs = [(cps - 1 - step_c) if it[1] == 1 else step_c for it in items]
            rs = [slice(c * CHUNK, (c + 1) * CHUNK) for c in cs]
            ws = [jnp.dot(wq_s[slot, items[i][0], items[i][1], cs[i]], st[i].astype(BF16),
                          preferred_element_type=F32) for i in idx]
            yield
            v_new = [(u_s[slot, items[i][0], items[i][1], rs[i], :] - ws[i][:CHUNK]).astype(BF16) for i in idx]
            st = [st[i] * gl[i][cs[i]:cs[i] + 1, :]
                  + jnp.dot(kdt_s[slot, items[i][0], items[i][1], :, rs[i]], v_new[i], preferred_element_type=F32)
                  for i in idx]
            if mode is not None:
                for i in idx:
                    hh, d, sc = items[i]
                    o = ws[i][CHUNK:] + jnp.dot(at_s[slot, hh, d, rs[i], :], v_new[i], preferred_element_type=F32)
                    ro = pl.multiple_of(sc * SUPER - n_ctx + cs[i] * CHUNK, CHUNK)
                    if mode == "set":
                        o_ref[0, pl.ds(ro, CHUNK), hh * DH:(hh + 1) * DH] = o
                    else:
                        o_ref[0, pl.ds(ro, CHUNK), hh * DH:(hh + 1) * DH] += o
            yield
        for i in idx:
            st_s[items[i][0], items[i][1]] = st[i]

    def interleave(*gens):
        live = list(gens)
        while live:
            for gen in list(live):
                try:
                    next(gen)
                except StopIteration:
                    live.remove(gen)

    def lat_ab(sc):
        return abl_ref[0, pl.ds(pl.multiple_of((sc - 1) * SUPER, SUPER), SUPER), :]

    def step(t, mode, with_prep, with_conv):
        slot = t % 2
        bwd_sc = (ns_lat + 1 - t) if mode is not None else 0
        gens = [scan([(hh, d, t if d == 0 else bwd_sc) for hh in range(HP) for d in range(2)], slot, mode)]
        if with_prep:
            ab_f, ab_b = lat_ab(t + 1), lat_ab(ns_lat - t)
            gens.append(prep([(hh, d, ab_f if d == 0 else ab_b, (t + 1) if d == 0 else (ns_lat - t))
                              for hh in range(HP) for d in range(2)], 1 - slot))
        if with_conv:
            gens.append(conv([(t + 2, False, False), (ns_lat - t - 1, False, False)]))
        interleave(*gens)

    st_s[...] = jnp.zeros_like(st_s)
    interleave(conv([(0, True, True)]))
    ab_c = abc_ref[0]
    interleave(prep([(hh, d, ab_c, 0) for hh in range(HP) for d in range(2)], 0),
               conv([(1, True, False), (ns_lat, False, True)]))
    step(0, None, True, True)
    half = ns_lat // 2
    conv_steps = (ns_lat - 2) // 2

    def run(lo, hi, mode, with_conv):
        def body(t, _):
            step(t, mode, True, with_conv)
            return 0
        lax.fori_loop(lo, hi, body, 0)

    run(1, conv_steps, "set", True)
    run(conv_steps, half + 1, "set", False)
    run(half + 1, ns_lat, "add", False)
    step(ns_lat, "add", False, False)


def _gdn(aqkv_lat, aqkv_ctx, ab_lat, ab_ctx, conv_w):
    bsz, n_lat, _ = aqkv_lat.shape
    n_ctx = aqkv_ctx.shape[1]
    n_all = n_ctx + n_lat
    width = HP * DH
    groups = HEADS // HP
    cps = SUPER // CHUNK

    def stream(n, j0):
        return pl.BlockSpec((1, n, width), lambda b, p: (b, 0, j0 + p))

    def wspec(j0):
        return pl.BlockSpec((CONV_K, width), lambda b, p: (0, j0 + p))

    return pl.pallas_call(
        functools.partial(_gdn_kernel, n_ctx=n_ctx, n_lat=n_lat),
        grid=(bsz, groups),
        in_specs=[stream(n_lat, 0), stream(n_lat, groups), stream(n_lat, 2 * groups),
                  stream(n_ctx, 0), stream(n_ctx, groups), stream(n_ctx, 2 * groups),
                  pl.BlockSpec((1, n_lat, 128), lambda b, p: (b, 0, 0)),
                  pl.BlockSpec((1, n_ctx, 128), lambda b, p: (b, 0, 0)),
                  wspec(0), wspec(groups), wspec(2 * groups)],
        out_specs=pl.BlockSpec((1, n_lat, width), lambda b, p: (b, 0, p)),
        out_shape=jax.ShapeDtypeStruct((bsz, n_lat, QK), F32),
        scratch_shapes=[pltpu.VMEM((n_all, width), F32), pltpu.VMEM((n_all, width), F32),
                        pltpu.VMEM((n_all, width), F32),
                        pltpu.VMEM((2, HP, 2, cps, 2 * CHUNK, DH), BF16),
                        pltpu.VMEM((2, HP, 2, SUPER, DH), F32),
                        pltpu.VMEM((2, HP, 2, DH, SUPER), BF16),
                        pltpu.VMEM((2, HP, 2, SUPER, CHUNK), BF16),
                        pltpu.VMEM((2, HP, 2, 8, DH), F32),
                        pltpu.VMEM((HP, 2, DH, DH), F32)],
        compiler_params=pltpu.CompilerParams(vmem_limit_bytes=VMEM_LIMIT),
        name="gdn",
    )(aqkv_lat, aqkv_lat, aqkv_lat, aqkv_ctx, aqkv_ctx, aqkv_ctx, ab_lat, ab_ctx,
      conv_w, conv_w, conv_w)


def _gla_kernel(ql_ref, gfl_ref, gbl_ref, kfl_ref, kbl_ref, vl_ref,
                qc_ref, gfc_ref, gbc_ref, kfc_ref, kbc_ref, vc_ref,
                o_ref, st_s, *, n_ctx, n_lat):
    assert n_ctx == SUPER and n_lat % (2 * SUPER) == 0
    ns_lat = n_lat // SUPER
    cps = SUPER // CHUNK

    def scan_supers(items, mode):
        idx = range(len(items))
        revs = [it[2] == 1 for it in items]
        masks = {d: _scan_masks(d == 1) for d in {it[2] for it in items}}
        r0 = [pl.multiple_of(it[3] * SUPER, SUPER) for it in items]
        hs = [slice(it[1] * DH, (it[1] + 1) * DH) for it in items]
        g = [items[i][0][1][0, pl.ds(r0[i], SUPER), hs[i]] for i in idx]
        k = [items[i][0][2][0, pl.ds(r0[i], SUPER), hs[i]].astype(F32) for i in idx]
        v = [items[i][0][3][0, pl.ds(r0[i], SUPER), hs[i]] for i in idx]
        lower = jnp.where(_scan_masks(False)[3], 1.0, 0.0)
        pre = {hh: _dot_split2(lower, jnp.concatenate([g[i] for i in idx if items[i][1] == hh], axis=1))
               for hh in sorted({it[1] for it in items})}
        yield
        gcum = [None] * len(items)
        for hh, pr in pre.items():
            for n, i in enumerate([i for i in idx if items[i][1] == hh]):
                p = pr[:, n * DH:(n + 1) * DH]
                gcum[i] = (_chunk_rows(p, CHUNK - 1, CHUNK) - p + g[i]) if revs[i] else p
        g_mid = [_chunk_rows(gcum[i], SUB // 2 if revs[i] else SUB // 2 - 1, SUB) for i in idx]
        g_tot = [_chunk_rows(gcum[i], 0 if revs[i] else CHUNK - 1, CHUNK) for i in idx]
        kd = [k[i] * jnp.exp(g_mid[i] - gcum[i]) for i in idx]
        k_out = [kd[i] * jnp.exp(g_tot[i] - g_mid[i]) for i in idx]
        if mode is not None:
            q = [items[i][0][0][0, pl.ds(r0[i], SUPER), hs[i]].astype(F32) for i in idx]
            row = lax.broadcasted_iota(jnp.int32, (SUPER, DH), 0) & (CHUNK - 1)
            first = [(row >= SUB) if revs[i] else (row < SUB) for i in idx]
            g_bnd = [_chunk_rows(gcum[i], SUB if revs[i] else SUB - 1, CHUNK) for i in idx]
            qd = [q[i] * jnp.exp(gcum[i] - g_mid[i]) for i in idx]
            e_b = [jnp.exp(jnp.where(first[i], g_bnd[i] - g_mid[i], g_mid[i] - g_bnd[i])) for i in idx]
            qo = [jnp.where(first[i], 0.0, qd[i] * e_b[i]) for i in idx]
            ko = [jnp.where(first[i], kd[i] * e_b[i], 0.0) for i in idx]
            sc1 = [_dot_nt(qd[i], kd[i]) for i in idx]
            sc2 = [_dot_nt(qo[i], ko[i]) for i in idx]
            q_in = [qd[i] * jnp.exp(g_mid[i]) for i in idx]
            yield
            attn = []
            for i in idx:
                ii, jj, same, incl, _ = masks[items[i][2]]
                attn.append(jnp.where(incl & ((ii ^ jj) < SUB), sc1[i], 0.0) + jnp.where(same, sc2[i], 0.0))
            o_intra = [_dot(attn[i], v[i]) for i in idx]
        d_st = [[_dot(jnp.transpose(v[i][c * CHUNK:(c + 1) * CHUNK].astype(F32)), k_out[i][c * CHUNK:(c + 1) * CHUNK])
                 for c in range(cps)] for i in idx]
        yield
        st = [st_s[it[1], it[2]] for it in items]
        st_in = [[None] * cps for _ in idx]
        for step_c in range(cps):
            for i in idx:
                c = (cps - 1 - step_c) if revs[i] else step_c
                st_in[i][c] = st[i]
                st[i] = st[i] * jnp.exp(g_tot[i][c * CHUNK:c * CHUNK + 1, :]) + d_st[i][c]
        outs = None
        if mode is not None:
            outs = [[o_intra[i][c * CHUNK:(c + 1) * CHUNK] + _dot_nt(q_in[i][c * CHUNK:(c + 1) * CHUNK], st_in[i][c])
                     for c in range(cps)] for i in idx]
        for i in idx:
            _, hh, d, _ = items[i]
            st_s[hh, d] = st[i]
            if mode == "set":
                o_ref[0, pl.ds(r0[i], SUPER), hs[i]] = jnp.concatenate(outs[i], axis=0)
            elif mode == "add":
                o_ref[0, pl.ds(r0[i], SUPER), hs[i]] += jnp.concatenate(outs[i], axis=0)

    def interleave(*gens):
        live = list(gens)
        while live:
            for gen in list(live):
                try:
                    next(gen)
                except StopIteration:
                    live.remove(gen)

    lat = ((ql_ref, gfl_ref, kfl_ref, vl_ref), (ql_ref, gbl_ref, kbl_ref, vl_ref))
    ctx = ((qc_ref, gfc_ref, kfc_ref, vc_ref), (qc_ref, gbc_ref, kbc_ref, vc_ref))
    st_s[...] = jnp.zeros_like(st_s)
    interleave(scan_supers([(ctx[d], hh, d, 0) for hh in range(HP) for d in range(2)], None))

    def run(lo, hi, mode):
        def body(n, _):
            interleave(*[scan_supers([(lat[d], hh, d, (2 * n + e) if d == 0 else (ns_lat - 1 - 2 * n - e))
                                      for hh in range(HP) for d in range(2)], mode) for e in range(2)])
            return 0
        lax.fori_loop(lo, hi, body, 0)

    assert ns_lat % 4 == 0
    run(0, ns_lat // 4, "set")
    run(ns_lat // 4, ns_lat // 2, "add")


def _gla(lat, ctx):
    bsz, n_lat, _ = lat[0].shape
    n_ctx = ctx[0].shape[1]
    width = HP * DH
    groups = HEADS // HP

    def specs(n):
        one = lambda j0: pl.BlockSpec((1, n, width), lambda b, p: (b, 0, j0 + p))
        return [one(0), one(0), one(groups), one(0), one(groups), one(0)]

    def args(t):
        q, g, k, v = t
        return [q, g, g, k, k, v]

    return pl.pallas_call(
        functools.partial(_gla_kernel, n_ctx=n_ctx, n_lat=n_lat),
        grid=(bsz, groups),
        in_specs=specs(n_lat) + specs(n_ctx),
        out_specs=pl.BlockSpec((1, n_lat, width), lambda b, p: (b, 0, p)),
        out_shape=jax.ShapeDtypeStruct((bsz, n_lat, QK), F32),
        scratch_shapes=[pltpu.VMEM((HP, 2, DH, DH), F32)],
        compiler_params=pltpu.CompilerParams(vmem_limit_bytes=VMEM_LIMIT),
        name="gla",
    )(*args(lat), *args(ctx))


def _gated_rms(o, gain, gate):
    parts = []
    for hh in range(HEADS):
        oh = o[:, hh * DH:(hh + 1) * DH]
        ms = jnp.mean(oh * oh, axis=-1, keepdims=True)
        parts.append(oh * lax.rsqrt(ms + RMS_EPS))
    return jnp.concatenate(parts, axis=1) * gain * _silu(gate)


def _post_kernel(x_ref, oa_ref, ob_ref, shift_ref, scale_ref, gate_ref, w_ref, wa_ref, wb_ref, wo_ref,
                 ga_ref, gb_ref, lng_ref, lnb_ref, out_ref, ob_s):
    for cl in range(COLS_PER_TILE):
        ob_s[:, cl, :] = ob_ref[0, cl]

    def part_stages(part):
        rows = GRID_W // POST_PARTS
        xt = _load_raster(x_ref, True, part, POST_PARTS)
        u = (_layer_norm(xt) * (1.0 + scale_ref[0]) + shift_ref[0]).astype(BF16)
        yield
        za = jnp.dot(u, w_ref[:, 0:QK], preferred_element_type=F32)
        y_a = _dot(_gated_rms(_load_raster(oa_ref, True, part, POST_PARTS), ga_ref[...], za), wa_ref[...])
        yield
        zb = jnp.dot(u, w_ref[:, QK:2 * QK], preferred_element_type=F32)
        ob = ob_s[part * rows:(part + 1) * rows].reshape(TOK_TILE // POST_PARTS, QK)
        y_b = _dot(_gated_rms(ob, gb_ref[...], zb), wb_ref[...])
        yield
        mix = _sigmoid(jnp.dot(u, w_ref[:, 2 * QK:2 * QK + D_MODEL], preferred_element_type=F32)) * y_a
        yield
        mix = mix + _sigmoid(jnp.dot(u, w_ref[:, 2 * QK + D_MODEL:], preferred_element_type=F32)) * y_b
        yield
        sub = _dot(mix, wo_ref[...])
        hres = DEEPNORM_ALPHA * xt + gate_ref[0] * sub
        y = _layer_norm(hres) * lng_ref[...] + lnb_ref[...]
        _store_raster(out_ref, y, True, part, POST_PARTS)

    _interleave_skewed([part_stages(p) for p in range(POST_PARTS)])


def _post_project(x, oa, ob_cm, shift, scale, gate, w_post, w_a_out, w_b_out, w_out, a_gain, b_gain, ln_g, ln_b):
    bsz, length, _ = x.shape
    rows = length // GRID_W
    nj = GRID_W // COLS_PER_TILE
    const = lambda shape: pl.BlockSpec(shape, lambda b, j: tuple(0 for _ in shape))
    rast = lambda width: pl.BlockSpec((1, rows, COLS_PER_TILE, width), lambda b, j: (b, 0, j, 0))
    modv = pl.BlockSpec((1, 1, D_MODEL), lambda b, j: (b, 0, 0))
    out = pl.pallas_call(
        _post_kernel,
        grid=(bsz, nj),
        in_specs=[rast(D_MODEL), rast(QK),
                  pl.BlockSpec((1, COLS_PER_TILE, rows, QK), lambda b, j: (b, j, 0, 0)),
                  modv, modv, modv,
                  const((D_MODEL, N_POST)), const((QK, D_MODEL)), const((QK, D_MODEL)),
                  const((D_MODEL, D_MODEL)), const((1, QK)), const((1, QK)),
                  const((1, D_MODEL)), const((1, D_MODEL))],
        out_specs=rast(D_MODEL),
        out_shape=jax.ShapeDtypeStruct((bsz, rows, GRID_W, D_MODEL), F32),
        scratch_shapes=[pltpu.VMEM((rows, COLS_PER_TILE, QK), F32)],
        compiler_params=pltpu.CompilerParams(vmem_limit_bytes=VMEM_LIMIT),
        name="post",
    )(x.reshape(bsz, rows, GRID_W, D_MODEL), oa.reshape(bsz, rows, GRID_W, QK),
      ob_cm.reshape(bsz, GRID_W, rows, QK), shift, scale, gate,
      w_post, w_a_out, w_b_out, w_out, a_gain, b_gain, ln_g, ln_b)
    return out.reshape(bsz, length, D_MODEL)


def kernel(x, c, ctx, c_ctx, w_mod, b_mod, w_in, conv_w, a_log, dt_bias, lb_param, a_norm_g, b_norm_g,
           w_a_out, w_b_out, w_out, ln_g, ln_b):
    assert w_mod.shape[0] == DEPTH
    bsz, length, _ = x.shape
    n_ctx = ctx.shape[1]
    f32 = lambda t: t.astype(F32)

    w = f32(w_in[0])
    o_alpha = 3 * QK
    o_agate = o_alpha + 4 * HEADS
    o_bq = o_agate + QK
    o_bgate = o_bq + 4 * QK
    w_pre = jnp.concatenate([w[:, :o_alpha], jnp.pad(w[:, o_alpha:o_agate], ((0, 0), (0, 128 - 4 * HEADS))),
                             w[:, o_bq:o_bgate]], axis=1).astype(BF16)
    w_post = jnp.concatenate([w[:, o_agate:o_bq], w[:, o_bgate:]], axis=1).astype(BF16)
    alog_vec = jnp.pad(f32(a_log[0]).reshape(1, 2 * HEADS), ((0, 0), (0, 128 - 2 * HEADS)))
    dtb_vec = jnp.pad(f32(dt_bias[0]).reshape(1, 2 * HEADS), ((0, 0), (0, 128 - 2 * HEADS)))
    lbp = f32(lb_param).reshape(DEPTH + 1, 2 * QK)

    mod = _modulation(f32(c), f32(c_ctx), f32(w_mod[0]), f32(b_mod[0]))
    shift_l = mod[:bsz, None, 0:D_MODEL]
    scale_l = mod[:bsz, None, D_MODEL:2 * D_MODEL]
    gate_l = mod[:bsz, None, 2 * D_MODEL:]
    shift_c = mod[bsz:bsz + 1, None, 0:D_MODEL]
    scale_c = mod[bsz:bsz + 1, None, D_MODEL:2 * D_MODEL]

    lat = _pre_project(f32(x), shift_l, scale_l, w_pre, alog_vec, dtb_vec, lbp, cm=True)
    cx = _pre_project(f32(ctx).reshape(bsz * n_ctx, D_MODEL), shift_c, scale_c, w_pre, alog_vec, dtb_vec, lbp,
                      cm=False)
    aqkv_l = lat[0].reshape(bsz, length, 3 * QK)
    ab_l = lat[1].reshape(bsz, length, 128)
    gla_l = tuple(t.reshape(bsz, length, t.shape[-1]) for t in lat[2:])
    aqkv_c = cx[0].reshape(bsz, n_ctx, 3 * QK)
    ab_c = cx[1].reshape(bsz, n_ctx, 128)
    gla_c = tuple(t.reshape(bsz, n_ctx, t.shape[-1]) for t in cx[2:])

    oa = _gdn(aqkv_l, aqkv_c, ab_l, ab_c, f32(conv_w[0]))
    ob_cm = _gla(gla_l, gla_c)

    a_gain = jnp.tile(f32(a_norm_g[0]), HEADS).reshape(1, QK)
    b_gain = jnp.tile(f32(b_norm_g[0]), HEADS).reshape(1, QK)
    out = _post_project(f32(x), oa, ob_cm, shift_l, scale_l, gate_l, w_post,
                        f32(w_a_out[0]).astype(BF16), f32(w_b_out[0]).astype(BF16), f32(w_out[0]).astype(BF16),
                        a_gain, b_gain, f32(ln_g[0]).reshape(1, D_MODEL), f32(ln_b[0]).reshape(1, D_MODEL))
    return out.astype(x.dtype)
```

```python
import functools

import jax
import jax.numpy as jnp
from jax import lax
from jax.experimental import pallas as pl
from jax.experimental.pallas import tpu as pltpu

D_MODEL = 1024
GRID_W = 64
HEADS = 4
DH = 128
QK = HEADS * DH
CONV_K = 5
CHUNK = 64
SUPER = 256
HALF = 128
SUB = 32
HP = 2
GB = 2
DEPTH = 1
DEEPNORM_ALPHA = (2 * DEPTH) ** 0.25
LN_EPS = 1e-6
RMS_EPS = 1e-6
L2_EPS = 1e-6

TOK_TILE = 512
COLS_PER_TILE = TOK_TILE // GRID_W
PRE_PARTS = 4
POST_PARTS = 2
N_PRE = 3 * QK + 128 + 4 * QK
N_POST = 2 * QK + 2 * D_MODEL
VMEM_LIMIT = 58 * 1024 * 1024

F32 = jnp.float32
BF16 = jnp.bfloat16


def _dot(a, b):
    return jnp.dot(a.astype(BF16), b.astype(BF16), preferred_element_type=F32)


def _dot_nt(a, b):
    return lax.dot_general(a.astype(BF16), b.astype(BF16), (((1,), (1,)), ((), ())),
                           preferred_element_type=F32)


def _dot_exact(a, b):
    return jnp.dot(a, b, preferred_element_type=F32, precision=lax.Precision.HIGHEST)


def _sigmoid(x):
    return 1.0 / (1.0 + jnp.exp(-x))


def _sigmoid_pair(x):
    t = jnp.exp(-jnp.abs(x))
    r = 1.0 / (1.0 + t)
    tr = t * r
    pos = x >= 0
    return jnp.where(pos, r, tr), jnp.where(pos, tr, r)


def _silu(x):
    return x * _sigmoid(x)


def _layer_norm(t):
    mu = jnp.mean(t, axis=-1, keepdims=True)
    tc = t - mu
    var = jnp.mean(tc * tc, axis=-1, keepdims=True)
    return tc * lax.rsqrt(var + LN_EPS)


def _mod_kernel(c_ref, w_ref, b_ref, o_ref):
    o_ref[...] = _dot_exact(_silu(c_ref[...]), w_ref[...]) + b_ref[...]


def _modulation(c, c_ctx, w_mod, b_mod):
    bsz = c.shape[0]
    rows = 16
    cc = jnp.zeros((rows, D_MODEL), F32).at[:bsz].set(c).at[bsz].set(c_ctx)
    nblk = 3
    out = pl.pallas_call(
        _mod_kernel,
        grid=(nblk,),
        in_specs=[pl.BlockSpec((rows, D_MODEL), lambda j: (0, 0)),
                  pl.BlockSpec((D_MODEL, D_MODEL), lambda j: (0, j)),
                  pl.BlockSpec((1, D_MODEL), lambda j: (0, j))],
        out_specs=pl.BlockSpec((rows, D_MODEL), lambda j: (0, j)),
        out_shape=jax.ShapeDtypeStruct((rows, 3 * D_MODEL), F32),
        name="mod",
    )(cc, w_mod, b_mod.reshape(1, 3 * D_MODEL))
    return out


def _interleave_skewed(gens):
    live = list(enumerate(gens))
    rnd = 0
    while live:
        for i, gen in list(live):
            if rnd >= i:
                try:
                    next(gen)
                except StopIteration:
                    live.remove((i, gen))
        rnd += 1


def _load_raster(ref, grid_tile, part, parts):
    tok = TOK_TILE // parts
    if grid_tile:
        rows = GRID_W // parts
        return ref[0, part * rows:(part + 1) * rows].reshape(tok, ref.shape[-1])
    return ref[part * tok:(part + 1) * tok, :]


def _load_colmajor(ref, part, parts):
    cols = COLS_PER_TILE // parts
    return jnp.concatenate([ref[0, :, cl, :] for cl in range(part * cols, (part + 1) * cols)], axis=0)


def _store_raster(ref, val, grid_tile, part, parts, lanes=slice(None)):
    tok = TOK_TILE // parts
    if grid_tile:
        rows = GRID_W // parts
        ref[0, part * rows:(part + 1) * rows, :, lanes] = (
            val.reshape(rows, COLS_PER_TILE, val.shape[-1]).astype(ref.dtype))
    else:
        ref[part * tok:(part + 1) * tok, lanes] = val.astype(ref.dtype)


def _store_colmajor(ref, val, grid_tile, part, parts, lanes=slice(None)):
    tok = TOK_TILE // parts
    if grid_tile:
        cols = COLS_PER_TILE // parts
        for n, cl in enumerate(range(part * cols, (part + 1) * cols)):
            ref[0, cl, :, lanes] = val[n * GRID_W:(n + 1) * GRID_W, :].astype(ref.dtype)
    else:
        ref[part * tok:(part + 1) * tok, lanes] = val.astype(ref.dtype)


def _pre_kernel(x_ref, shift_ref, scale_ref, w_ref, alog_ref, dtb_ref, lbp_ref,
                aqkv_ref, ab_ref, bq_ref, bg_ref, bk_ref, bi_ref, *, cm):
    modulate = lambda t: (_layer_norm(t) * (1.0 + scale_ref[0]) + shift_ref[0]).astype(BF16)

    def part_stages(part):
        u = modulate(_load_raster(x_ref, cm, part, PRE_PARTS))
        yield
        for g in range(3):
            z = jnp.dot(u, w_ref[:, g * QK:(g + 1) * QK], preferred_element_type=F32)
            _store_raster(aqkv_ref, z, cm, part, PRE_PARTS, slice(g * QK, (g + 1) * QK))
            yield
        off = 3 * QK
        z = jnp.dot(u, w_ref[:, off:off + 128], preferred_element_type=F32)
        zs = z + dtb_ref[...]
        softplus = jnp.maximum(zs, 0.0) + jnp.log(1.0 + jnp.exp(-jnp.abs(zs)))
        a_g = -jnp.exp(alog_ref[...]) * softplus
        a_b = _sigmoid(z)
        lane = lax.broadcasted_iota(jnp.int32, z.shape, 1)
        _store_raster(ab_ref, jnp.where(lane < 2 * HEADS, a_g, a_b), cm, part, PRE_PARTS)
        u_b = modulate(_load_colmajor(x_ref, part, PRE_PARTS)) if cm else u
        yield
        off += 128
        z = jnp.dot(u_b, w_ref[:, off:off + QK], preferred_element_type=F32)
        _store_colmajor(bq_ref, _silu(z) * DH ** -0.5, cm, part, PRE_PARTS)
        yield
        off += QK
        p0 = lbp_ref[0:1, :]
        p1 = lbp_ref[1:2, :]
        pm = jnp.maximum(p0, p1)
        e0 = jnp.exp(p0 - pm)
        lb = e0 / (e0 + jnp.exp(p1 - pm))
        for d in range(2):
            z = jnp.dot(u_b, w_ref[:, off + d * QK: off + (d + 1) * QK], preferred_element_type=F32)
            s_pos, s_neg = _sigmoid_pair(z)
            lbd = lb[:, d * QK:(d + 1) * QK]
            lanes = slice(d * QK, (d + 1) * QK)
            _store_colmajor(bg_ref, jnp.log(lbd + (1.0 - lbd) * s_pos), cm, part, PRE_PARTS, lanes)
            _store_colmajor(bk_ref, (1.0 - lbd) * s_neg, cm, part, PRE_PARTS, lanes)
            yield
        off += 2 * QK
        z = jnp.dot(u_b, w_ref[:, off:off + QK], preferred_element_type=F32)
        _store_colmajor(bi_ref, z, cm, part, PRE_PARTS)

    _interleave_skewed([part_stages(p) for p in range(PRE_PARTS)])


def _pre_project(tokens, shift, scale, w_pre, alog_vec, dtb_vec, lbp, *, cm):
    consts = [pl.BlockSpec((D_MODEL, N_PRE), lambda *_: (0, 0)),
              pl.BlockSpec((1, 128), lambda *_: (0, 0)),
              pl.BlockSpec((1, 128), lambda *_: (0, 0)),
              pl.BlockSpec((2, 2 * QK), lambda *_: (0, 0))]
    if cm:
        bsz, length, _ = tokens.shape
        rows = length // GRID_W
        assert rows == GRID_W and GRID_W % COLS_PER_TILE == 0
        nj = GRID_W // COLS_PER_TILE
        x_view = tokens.reshape(bsz, rows, GRID_W, D_MODEL)
        grid = (bsz, nj)
        in_specs = [pl.BlockSpec((1, rows, COLS_PER_TILE, D_MODEL), lambda b, j: (b, 0, j, 0)),
                    pl.BlockSpec((1, 1, D_MODEL), lambda b, j: (b, 0, 0)),
                    pl.BlockSpec((1, 1, D_MODEL), lambda b, j: (b, 0, 0))] + consts

        def rast(width, dtype):
            return (jax.ShapeDtypeStruct((bsz, rows, GRID_W, width), dtype),
                    pl.BlockSpec((1, rows, COLS_PER_TILE, width), lambda b, j: (b, 0, j, 0)))

        def colm(width, dtype):
            return (jax.ShapeDtypeStruct((bsz, GRID_W, rows, width), dtype),
                    pl.BlockSpec((1, COLS_PER_TILE, rows, width), lambda b, j: (b, j, 0, 0)))
    else:
        n_tok = tokens.shape[0]
        assert n_tok % TOK_TILE == 0
        x_view = tokens
        grid = (n_tok // TOK_TILE,)
        in_specs = [pl.BlockSpec((TOK_TILE, D_MODEL), lambda i: (i, 0)),
                    pl.BlockSpec((1, 1, D_MODEL), lambda i: (0, 0, 0)),
                    pl.BlockSpec((1, 1, D_MODEL), lambda i: (0, 0, 0))] + consts

        def rast(width, dtype):
            return (jax.ShapeDtypeStruct((n_tok, width), dtype),
                    pl.BlockSpec((TOK_TILE, width), lambda i: (i, 0)))

        colm = rast

    outs = [rast(3 * QK, BF16), rast(128, F32), colm(QK, BF16), colm(2 * QK, F32),
            colm(2 * QK, BF16), colm(QK, BF16)]
    res = pl.pallas_call(
        functools.partial(_pre_kernel, cm=cm),
        grid=grid,
        in_specs=in_specs,
        out_specs=[o[1] for o in outs],
        out_shape=[o[0] for o in outs],
        compiler_params=pltpu.CompilerParams(vmem_limit_bytes=VMEM_LIMIT),
        name="pre_lat" if cm else "pre_ctx",
    )(x_view, shift, scale, w_pre, alog_vec, dtb_vec, lbp)
    return res


def _iota2(n):
    return (lax.broadcasted_iota(jnp.int32, (n, n), 0), lax.broadcasted_iota(jnp.int32, (n, n), 1))


def _scan_masks(rev):
    ii, jj = _iota2(SUPER)
    same = (ii ^ jj) < CHUNK
    incl = same & ((jj >= ii) if rev else (jj <= ii))
    strict = same & ((jj > ii) if rev else (jj < ii))
    return ii, jj, same, incl, strict


def _dot_split2(m01, g):
    m = m01.astype(BF16)
    g1 = g.astype(BF16)
    g2 = (g - g1.astype(F32)).astype(BF16)
    return jnp.dot(m, g1, preferred_element_type=F32) + jnp.dot(m, g2, preferred_element_type=F32)


def _chunk_rows(t, offset, span):
    return jnp.concatenate([jnp.broadcast_to(t[span * i + offset: span * i + offset + 1, :], (span, t.shape[1]))
                            for i in range(SUPER // span)], axis=0)


def _gdn_kernel(ql_ref, kl_ref, vl_ref, qc_ref, kc_ref, vc_ref, abl_ref, abc_ref,
                wq_ref, wk_ref, wv_ref, o_ref,
                q_s, k_s, v_s, wq_s, u_s, kdt_s, at_s, gl_s, st_s, *, n_ctx, n_lat):
    head0 = pl.program_id(1) * HP
    assert n_ctx == SUPER and n_lat % (2 * SUPER) == 0
    ns_lat = n_lat // SUPER
    cps = SUPER // CHUNK
    width = HP * DH

    halo = 16

    def conv_block(kind, src_ref, w_ref, dst_ref, src_r0, dst_r0, left_edge, right_edge):
        lo = 0 if left_edge else halo
        hi = 0 if right_edge else halo
        start = src_r0 - lo if isinstance(src_r0, int) else pl.multiple_of(src_r0 - lo, halo)
        parts = [src_ref[0, pl.ds(start, SUPER + lo + hi), :].astype(F32)]
        if left_edge:
            parts.insert(0, jnp.zeros((halo, width), F32))
        if right_edge:
            parts.append(jnp.zeros((halo, width), F32))
        xv = jnp.concatenate(parts, axis=0) if len(parts) > 1 else parts[0]
        acc = jnp.zeros((SUPER, width), F32)
        for j in range(CONV_K):
            sh = (CONV_K // 2 - j) % (SUPER + 2 * halo)
            tap = xv if sh == 0 else pltpu.roll(xv, sh, 0)
            acc = acc + tap[halo:halo + SUPER, :] * w_ref[j:j + 1, :]
        y = _silu(acc)
        for hh in range(HP):
            yh = y[:, hh * DH:(hh + 1) * DH]
            if kind != "v":
                yh = yh * lax.rsqrt(jnp.sum(yh * yh, axis=-1, keepdims=True) + L2_EPS)
            if kind == "q":
                yh = yh * DH ** -0.5
            dst_ref[pl.ds(dst_r0, SUPER), hh * DH:(hh + 1) * DH] = yh.astype(dst_ref.dtype)

    streams = (("q", ql_ref, qc_ref, wq_ref, q_s), ("k", kl_ref, kc_ref, wk_ref, k_s),
               ("v", vl_ref, vc_ref, wv_ref, v_s))

    def conv(blocks):
        for sc, left_edge, right_edge in blocks:
            for kind, lat_ref, ctx_ref, w_ref, dst_ref in streams:
                if isinstance(sc, int) and sc == 0:
                    conv_block(kind, ctx_ref, w_ref, dst_ref, 0, 0, True, True)
                else:
                    conv_block(kind, lat_ref, w_ref, dst_ref, (sc - 1) * SUPER,
                               sc * SUPER if isinstance(sc, int) else pl.multiple_of(sc * SUPER, SUPER),
                               left_edge, right_edge)
                yield

    lane = lax.broadcasted_iota(jnp.int32, (SUPER, 128), 1)

    def prep(items, slot):
        idx = range(len(items))
        masks = {d: _scan_masks(d == 1) for d in {it[1] for it in items}}
        incl = [masks[it[1]][3] for it in items]
        strict = [masks[it[1]][4] for it in items]
        revs = [it[1] == 1 for it in items]
        r0 = [pl.multiple_of(it[3] * SUPER, SUPER) for it in items]
        hs = [slice(it[0] * DH, (it[0] + 1) * DH) for it in items]
        q = [q_s[pl.ds(r0[i], SUPER), hs[i]].astype(F32) for i in idx]
        k = [k_s[pl.ds(r0[i], SUPER), hs[i]].astype(F32) for i in idx]
        v = [v_s[pl.ds(r0[i], SUPER), hs[i]].astype(F32) for i in idx]
        kq = [_dot_nt(jnp.concatenate([k[i], q[i]], axis=0), k[i]) for i in idx]
        cols = [it[1] * HEADS + head0 + it[0] for it in items]
        pick = lambda ab, col: jnp.broadcast_to(
            jnp.sum(jnp.where(lane == col, ab, 0.0), axis=-1, keepdims=True), (SUPER, DH))
        beta = [pick(items[i][2], 2 * HEADS + cols[i]) for i in idx]
        lower = jnp.where(masks[0][3] if 0 in masks else _scan_masks(False)[3], 1.0, 0.0)
        prefix = {}
        for it in items:
            if id(it[2]) not in prefix:
                prefix[id(it[2])] = _dot_split2(lower, it[2])
        gcum = []
        for i in idx:
            pre = pick(prefix[id(items[i][2])], cols[i])
            if revs[i]:
                pre = _chunk_rows(pre, CHUNK - 1, CHUNK) - pre + pick(items[i][2], cols[i])
            gcum.append(pre)
        yield
        gtot = [_chunk_rows(gcum[i], 0 if revs[i] else CHUNK - 1, CHUNK) for i in idx]
        grow = [jnp.transpose(gcum[i])[0:1, :] for i in idx]
        decay = [jnp.where(incl[i], jnp.exp(jnp.where(
            incl[i], jnp.concatenate([gcum[i], gcum[i]], axis=1) - grow[i], 0.0)), 0.0) for i in idx]
        e_g = [jnp.exp(gcum[i]) for i in idx]
        a_mat = [jnp.where(strict[i], kq[i][:SUPER] * jnp.concatenate([beta[i], beta[i]], axis=1) * decay[i], 0.0)
                 for i in idx]
        attn = [kq[i][SUPER:] * decay[i] for i in idx]
        for c in range(cps):
            rs = slice(c * CHUNK, (c + 1) * CHUNK)
            for i in idx:
                at_s[slot, items[i][0], items[i][1], items[i][4], rs, :] = attn[i][rs, rs].astype(BF16)
        halves = [(i, h) for i in idx for h in range(SUPER // HALF)]
        a_half = [a_mat[i][h * HALF:(h + 1) * HALF, h * HALF:(h + 1) * HALF] for i, h in halves]
        ii, jj = _iota2(HALF)
        eye = jnp.where(ii == jj, 1.0, 0.0).astype(F32)
        t = None
        s = 1
        while s < CHUNK:
            a_off = []
            for n, (i, _) in enumerate(halves):
                row, col = (jj, ii) if revs[i] else (ii, jj)
                m = ((row ^ col) < 2 * s) & ((row & s) != 0) & ((col & s) == 0)
                a_off.append(jnp.where(m, a_half[n], 0.0))
            if t is None:
                t = [eye - a for a in a_off]
            else:
                p = [_dot(a_off[n], t[n]) for n in range(len(halves))]
                yield
                t = [t[n] - _dot(t[n], p[n]) for n in range(len(halves))]
                yield
            s *= 2
        rhs = [jnp.concatenate([k[i] * beta[i] * e_g[i], v[i] * beta[i]], axis=1) for i in idx]
        sol_half = [_dot(t[n], rhs[i][h * HALF:(h + 1) * HALF]) for n, (i, h) in enumerate(halves)]
        sol = [jnp.concatenate([sol_half[n] for n, (j, _) in enumerate(halves) if j == i], axis=0)
               for i in idx]
        qg = [q[i] * e_g[i] for i in idx]
        kdt = [jnp.transpose(k[i] * jnp.exp(gtot[i] - gcum[i])) for i in idx]
        yield
        for i in idx:
            hh, d, e = items[i][0], items[i][1], items[i][4]
            for c in range(cps):
                rs = slice(c * CHUNK, (c + 1) * CHUNK)
                wq_s[slot, hh, d, e, c] = jnp.concatenate([sol[i][rs, :DH], qg[i][rs, :]], axis=0).astype(BF16)
            u_s[slot, hh, d, e] = sol[i][:, DH:]
            kdt_s[slot, hh, d, e] = kdt[i].astype(BF16)
            gl_s[slot, hh, d, e] = jnp.concatenate(
                [jnp.exp(gtot[i][c * CHUNK:c * CHUNK + 1, :]) for c in range(cps)]
                + [jnp.zeros((8 - cps, DH), F32)], axis=0)

    def scan(chains, slot, mode):
        idx = range(len(chains))
        st = [st_s[ch[0], ch[1]] for ch in chains]
        for e in range(len(chains[0][2])):
            gl = [gl_s[slot, ch[0], ch[1], e] for ch in chains]
            for step_c in range(cps):
                cs = [(cps - 1 - step_c) if ch[1] == 1 else step_c for ch in chains]
                rs = [slice(c * CHUNK, (c + 1) * CHUNK) for c in cs]
                ws = [jnp.dot(wq_s[slot, chains[i][0], chains[i][1], e, cs[i]], st[i].astype(BF16),
                              preferred_element_type=F32) for i in idx]
                yield
                v_new = [(u_s[slot, chains[i][0], chains[i][1], e, rs[i], :] - ws[i][:CHUNK]).astype(BF16)
                         for i in idx]
                st = [st[i] * gl[i][cs[i]:cs[i] + 1, :]
                      + jnp.dot(kdt_s[slot, chains[i][0], chains[i][1], e, :, rs[i]], v_new[i],
                                preferred_element_type=F32) for i in idx]
                if mode is not None:
                    for i in idx:
                        hh, d, scs = chains[i]
                        o = ws[i][CHUNK:] + jnp.dot(at_s[slot, hh, d, e, rs[i], :], v_new[i],
                                                    preferred_element_type=F32)
                        ro = pl.multiple_of(scs[e] * SUPER - n_ctx + cs[i] * CHUNK, CHUNK)
                        if mode == "set":
                            o_ref[0, pl.ds(ro, CHUNK), hh * DH:(hh + 1) * DH] = o
                        else:
                            o_ref[0, pl.ds(ro, CHUNK), hh * DH:(hh + 1) * DH] += o
                yield
        for i in idx:
            st_s[chains[i][0], chains[i][1]] = st[i]

    def interleave(*gens):
        live = list(gens)
        while live:
            for gen in list(live):
                try:
                    next(gen)
                except StopIteration:
                    live.remove(gen)

    def lat_ab(sc):
        return abl_ref[0, pl.ds(pl.multiple_of((sc - 1) * SUPER, SUPER), SUPER), :]

    def group(u, d):
        if isinstance(u, int) and u == 0:
            return [0]
        first = GB * (u - 1)
        return [first + 1 + e if d == 0 else ns_lat - first - e for e in range(GB)]

    def step(u, mode, with_prep, with_conv):
        slot = u % 2
        gens = [scan([(hh, d, group(u, d)) for hh in range(HP) for d in range(2)], slot, mode)]
        if with_prep:
            nxt = [[(sc, lat_ab(sc)) for sc in group(u + 1, d)] for d in range(2)]
            gens.append(prep([(hh, d, nxt[d][e][1], nxt[d][e][0], e)
                              for hh in range(HP) for d in range(2) for e in range(GB)], 1 - slot))
        if with_conv:
            gens.append(conv([(sc, False, False) for d in range(2) for sc in group(u + 2, d)]))
        interleave(*gens)

    assert ns_lat % (2 * GB) == 0 and ns_lat >= 4 * GB
    n_groups = ns_lat // GB
    conv_steps = (ns_lat - 4 * GB) // (2 * GB) + 1
    st_s[...] = jnp.zeros_like(st_s)
    interleave(conv([(0, True, True)]))
    ab_c = abc_ref[0]
    edge = lambda sc: (sc, sc == 1, sc == ns_lat)
    interleave(prep([(hh, d, ab_c, 0, 0) for hh in range(HP) for d in range(2)], 0),
               conv([edge(sc) for d in range(2) for sc in group(1, d)]))
    step(0, None, True, True)

    def run(lo, hi, mode, with_conv):
        def body(u, _):
            step(u, mode, True, with_conv)
            return 0
        lax.fori_loop(lo, hi, body, 0)

    run(1, conv_steps, "set", True)
    run(conv_steps, n_groups // 2 + 1, "set", False)
    run(n_groups // 2 + 1, n_groups, "add", False)
    step(n_groups, "add", False, False)


def _gdn(aqkv_lat, aqkv_ctx, ab_lat, ab_ctx, conv_w):
    bsz, n_lat, _ = aqkv_lat.shape
    n_ctx = aqkv_ctx.shape[1]
    n_all = n_ctx + n_lat
    width = HP * DH
    groups = HEADS // HP
    cps = SUPER // CHUNK

    def stream(n, j0):
        return pl.BlockSpec((1, n, width), lambda b, p: (b, 0, j0 + p))

    def wspec(j0):
        return pl.BlockSpec((CONV_K, width), lambda b, p: (0, j0 + p))

    return pl.pallas_call(
        functools.partial(_gdn_kernel, n_ctx=n_ctx, n_lat=n_lat),
        grid=(bsz, groups),
        in_specs=[stream(n_lat, 0), stream(n_lat, groups), stream(n_lat, 2 * groups),
                  stream(n_ctx, 0), stream(n_ctx, groups), stream(n_ctx, 2 * groups),
                  pl.BlockSpec((1, n_lat, 128), lambda b, p: (b, 0, 0)),
                  pl.BlockSpec((1, n_ctx, 128), lambda b, p: (b, 0, 0)),
                  wspec(0), wspec(groups), wspec(2 * groups)],
        out_specs=pl.BlockSpec((1, n_lat, width), lambda b, p: (b, 0, p)),
        out_shape=jax.ShapeDtypeStruct((bsz, n_lat, QK), F32),
        scratch_shapes=[pltpu.VMEM((n_all, width), BF16), pltpu.VMEM((n_all, width), BF16),
                        pltpu.VMEM((n_all, width), BF16),
                        pltpu.VMEM((2, HP, 2, GB, cps, 2 * CHUNK, DH), BF16),
                        pltpu.VMEM((2, HP, 2, GB, SUPER, DH), F32),
                        pltpu.VMEM((2, HP, 2, GB, DH, SUPER), BF16),
                        pltpu.VMEM((2, HP, 2, GB, SUPER, CHUNK), BF16),
                        pltpu.VMEM((2, HP, 2, GB, 8, DH), F32),
                        pltpu.VMEM((HP, 2, DH, DH), F32)],
        compiler_params=pltpu.CompilerParams(vmem_limit_bytes=VMEM_LIMIT),
        name="gdn",
    )(aqkv_lat, aqkv_lat, aqkv_lat, aqkv_ctx, aqkv_ctx, aqkv_ctx, ab_lat, ab_ctx,
      conv_w, conv_w, conv_w)


def _gla_kernel(ql_ref, gfl_ref, gbl_ref, kfl_ref, kbl_ref, vl_ref,
                qc_ref, gfc_ref, gbc_ref, kfc_ref, kbc_ref, vc_ref,
                o_ref, st_s, *, n_ctx, n_lat):
    assert n_ctx == SUPER and n_lat % (2 * SUPER) == 0
    ns_lat = n_lat // SUPER
    cps = SUPER // CHUNK

    def scan_supers(items, mode):
        idx = range(len(items))
        revs = [it[2] == 1 for it in items]
        masks = {d: _scan_masks(d == 1) for d in {it[2] for it in items}}
        r0 = [pl.multiple_of(it[3] * SUPER, SUPER) for it in items]
        hs = [slice(it[1] * DH, (it[1] + 1) * DH) for it in items]
        g = [items[i][0][1][0, pl.ds(r0[i], SUPER), hs[i]] for i in idx]
        k = [items[i][0][2][0, pl.ds(r0[i], SUPER), hs[i]].astype(F32) for i in idx]
        v = [items[i][0][3][0, pl.ds(r0[i], SUPER), hs[i]] for i in idx]
        lower = jnp.where(_scan_masks(False)[3], 1.0, 0.0)
        pre = {hh: _dot_split2(lower, jnp.concatenate([g[i] for i in idx if items[i][1] == hh], axis=1))
               for hh in sorted({it[1] for it in items})}
        yield
        gcum = [None] * len(items)
        for hh, pr in pre.items():
            for n, i in enumerate([i for i in idx if items[i][1] == hh]):
                p = pr[:, n * DH:(n + 1) * DH]
                gcum[i] = (_chunk_rows(p, CHUNK - 1, CHUNK) - p + g[i]) if revs[i] else p
        g_mid = [_chunk_rows(gcum[i], SUB // 2 if revs[i] else SUB // 2 - 1, SUB) for i in idx]
        g_tot = [_chunk_rows(gcum[i], 0 if revs[i] else CHUNK - 1, CHUNK) for i in idx]
        kd = [k[i] * jnp.exp(g_mid[i] - gcum[i]) for i in idx]
        k_out = [kd[i] * jnp.exp(g_tot[i] - g_mid[i]) for i in idx]
        if mode is not None:
            q = [items[i][0][0][0, pl.ds(r0[i], SUPER), hs[i]].astype(F32) for i in idx]
            row = lax.broadcasted_iota(jnp.int32, (SUPER, DH), 0) & (CHUNK - 1)
            first = [(row >= SUB) if revs[i] else (row < SUB) for i in idx]
            g_bnd = [_chunk_rows(gcum[i], SUB if revs[i] else SUB - 1, CHUNK) for i in idx]
            qd = [q[i] * jnp.exp(gcum[i] - g_mid[i]) for i in idx]
            e_b = [jnp.exp(jnp.where(first[i], g_bnd[i] - g_mid[i], g_mid[i] - g_bnd[i])) for i in idx]
            qo = [jnp.where(first[i], 0.0, qd[i] * e_b[i]) for i in idx]
            ko = [jnp.where(first[i], kd[i] * e_b[i], 0.0) for i in idx]
            sc1 = [_dot_nt(qd[i], kd[i]) for i in idx]
            sc2 = [_dot_nt(qo[i], ko[i]) for i in idx]
            q_in = [qd[i] * jnp.exp(g_mid[i]) for i in idx]
            yield
            attn = []
            for i in idx:
                ii, jj, same, incl, _ = masks[items[i][2]]
                attn.append(jnp.where(incl & ((ii ^ jj) < SUB), sc1[i], 0.0) + jnp.where(same, sc2[i], 0.0))
            o_intra = [_dot(attn[i], v[i]) for i in idx]
        d_st = [[_dot(jnp.transpose(v[i][c * CHUNK:(c + 1) * CHUNK].astype(F32)), k_out[i][c * CHUNK:(c + 1) * CHUNK])
                 for c in range(cps)] for i in idx]
        yield
        st = [st_s[it[1], it[2]] for it in items]
        st_in = [[None] * cps for _ in idx]
        for step_c in range(cps):
            for i in idx:
                c = (cps - 1 - step_c) if revs[i] else step_c
                st_in[i][c] = st[i]
                st[i] = st[i] * jnp.exp(g_tot[i][c * CHUNK:c * CHUNK + 1, :]) + d_st[i][c]
        outs = None
        if mode is not None:
            outs = [[o_intra[i][c * CHUNK:(c + 1) * CHUNK] + _dot_nt(q_in[i][c * CHUNK:(c + 1) * CHUNK], st_in[i][c])
                     for c in range(cps)] for i in idx]
        for i in idx:
            _, hh, d, _ = items[i]
            st_s[hh, d] = st[i]
            if mode == "set":
                o_ref[0, pl.ds(r0[i], SUPER), hs[i]] = jnp.concatenate(outs[i], axis=0)
            elif mode == "add":
                o_ref[0, pl.ds(r0[i], SUPER), hs[i]] += jnp.concatenate(outs[i], axis=0)

    def interleave(*gens):
        live = list(gens)
        while live:
            for gen in list(live):
                try:
                    next(gen)
                except StopIteration:
                    live.remove(gen)

    lat = ((ql_ref, gfl_ref, kfl_ref, vl_ref), (ql_ref, gbl_ref, kbl_ref, vl_ref))
    ctx = ((qc_ref, gfc_ref, kfc_ref, vc_ref), (qc_ref, gbc_ref, kbc_ref, vc_ref))
    st_s[...] = jnp.zeros_like(st_s)
    interleave(scan_supers([(ctx[d], hh, d, 0) for hh in range(HP) for d in range(2)], None))

    def run(lo, hi, mode):
        def body(n, _):
            interleave(*[scan_supers([(lat[d], hh, d, (2 * n + e) if d == 0 else (ns_lat - 1 - 2 * n - e))
                                      for hh in range(HP) for d in range(2)], mode) for e in range(2)])
            return 0
        lax.fori_loop(lo, hi, body, 0)

    assert ns_lat % 4 == 0
    run(0, ns_lat // 4, "set")
    run(ns_lat // 4, ns_lat // 2, "add")


def _gla(lat, ctx):
    bsz, n_lat, _ = lat[0].shape
    n_ctx = ctx[0].shape[1]
    width = HP * DH
    groups = HEADS // HP

    def specs(n):
        one = lambda j0: pl.BlockSpec((1, n, width), lambda b, p: (b, 0, j0 + p))
        return [one(0), one(0), one(groups), one(0), one(groups), one(0)]

    def args(t):
        q, g, k, v = t
        return [q, g, g, k, k, v]

    return pl.pallas_call(
        functools.partial(_gla_kernel, n_ctx=n_ctx, n_lat=n_lat),
        grid=(bsz, groups),
        in_specs=specs(n_lat) + specs(n_ctx),
        out_specs=pl.BlockSpec((1, n_lat, width), lambda b, p: (b, 0, p)),
        out_shape=jax.ShapeDtypeStruct((bsz, n_lat, QK), F32),
        scratch_shapes=[pltpu.VMEM((HP, 2, DH, DH), F32)],
        compiler_params=pltpu.CompilerParams(vmem_limit_bytes=VMEM_LIMIT),
        name="gla",
    )(*args(lat), *args(ctx))


def _gated_rms(o, gain, gate):
    parts = []
    for hh in range(HEADS):
        oh = o[:, hh * DH:(hh + 1) * DH]
        ms = jnp.mean(oh * oh, axis=-1, keepdims=True)
        parts.append(oh * lax.rsqrt(ms + RMS_EPS))
    return jnp.concatenate(parts, axis=1) * gain * _silu(gate)


def _post_kernel(x_ref, oa_ref, ob_ref, shift_ref, scale_ref, gate_ref, w_ref, wa_ref, wb_ref, wo_ref,
                 ga_ref, gb_ref, lng_ref, lnb_ref, out_ref, ob_s):
    for cl in range(COLS_PER_TILE):
        ob_s[:, cl, :] = ob_ref[0, cl]

    def part_stages(part):
        rows = GRID_W // POST_PARTS
        xt = _load_raster(x_ref, True, part, POST_PARTS)
        u = (_layer_norm(xt) * (1.0 + scale_ref[0]) + shift_ref[0]).astype(BF16)
        yield
        za = jnp.dot(u, w_ref[:, 0:QK], preferred_element_type=F32)
        y_a = _dot(_gated_rms(_load_raster(oa_ref, True, part, POST_PARTS), ga_ref[...], za), wa_ref[...])
        yield
        zb = jnp.dot(u, w_ref[:, QK:2 * QK], preferred_element_type=F32)
        ob = ob_s[part * rows:(part + 1) * rows].reshape(TOK_TILE // POST_PARTS, QK)
        y_b = _dot(_gated_rms(ob, gb_ref[...], zb), wb_ref[...])
        yield
        mix = _sigmoid(jnp.dot(u, w_ref[:, 2 * QK:2 * QK + D_MODEL], preferred_element_type=F32)) * y_a
        yield
        mix = mix + _sigmoid(jnp.dot(u, w_ref[:, 2 * QK + D_MODEL:], preferred_element_type=F32)) * y_b
        yield
        sub = _dot(mix, wo_ref[...])
        hres = DEEPNORM_ALPHA * xt + gate_ref[0] * sub
        y = _layer_norm(hres) * lng_ref[...] + lnb_ref[...]
        _store_raster(out_ref, y, True, part, POST_PARTS)

    _interleave_skewed([part_stages(p) for p in range(POST_PARTS)])


def _post_project(x, oa, ob_cm, shift, scale, gate, w_post, w_a_out, w_b_out, w_out, a_gain, b_gain, ln_g, ln_b):
    bsz, length, _ = x.shape
    rows = length // GRID_W
    nj = GRID_W // COLS_PER_TILE
    const = lambda shape: pl.BlockSpec(shape, lambda b, j: tuple(0 for _ in shape))
    rast = lambda width: pl.BlockSpec((1, rows, COLS_PER_TILE, width), lambda b, j: (b, 0, j, 0))
    modv = pl.BlockSpec((1, 1, D_MODEL), lambda b, j: (b, 0, 0))
    out = pl.pallas_call(
        _post_kernel,
        grid=(bsz, nj),
        in_specs=[rast(D_MODEL), rast(QK),
                  pl.BlockSpec((1, COLS_PER_TILE, rows, QK), lambda b, j: (b, j, 0, 0)),
                  modv, modv, modv,
                  const((D_MODEL, N_POST)), const((QK, D_MODEL)), const((QK, D_MODEL)),
                  const((D_MODEL, D_MODEL)), const((1, QK)), const((1, QK)),
                  const((1, D_MODEL)), const((1, D_MODEL))],
        out_specs=rast(D_MODEL),
        out_shape=jax.ShapeDtypeStruct((bsz, rows, GRID_W, D_MODEL), F32),
        scratch_shapes=[pltpu.VMEM((rows, COLS_PER_TILE, QK), F32)],
        compiler_params=pltpu.CompilerParams(vmem_limit_bytes=VMEM_LIMIT),
        name="post",
    )(x.reshape(bsz, rows, GRID_W, D_MODEL), oa.reshape(bsz, rows, GRID_W, QK),
      ob_cm.reshape(bsz, GRID_W, rows, QK), shift, scale, gate,
      w_post, w_a_out, w_b_out, w_out, a_gain, b_gain, ln_g, ln_b)
    return out.reshape(bsz, length, D_MODEL)


def kernel(x, c, ctx, c_ctx, w_mod, b_mod, w_in, conv_w, a_log, dt_bias, lb_param, a_norm_g, b_norm_g,
           w_a_out, w_b_out, w_out, ln_g, ln_b):
    assert w_mod.shape[0] == DEPTH
    bsz, length, _ = x.shape
    n_ctx = ctx.shape[1]
    f32 = lambda t: t.astype(F32)

    w = f32(w_in[0])
    o_alpha = 3 * QK
    o_agate = o_alpha + 4 * HEADS
    o_bq = o_agate + QK
    o_bgate = o_bq + 4 * QK
    w_pre = jnp.concatenate([w[:, :o_alpha], jnp.pad(w[:, o_alpha:o_agate], ((0, 0), (0, 128 - 4 * HEADS))),
                             w[:, o_bq:o_bgate]], axis=1).astype(BF16)
    w_post = jnp.concatenate([w[:, o_agate:o_bq], w[:, o_bgate:]], axis=1).astype(BF16)
    alog_vec = jnp.pad(f32(a_log[0]).reshape(1, 2 * HEADS), ((0, 0), (0, 128 - 2 * HEADS)))
    dtb_vec = jnp.pad(f32(dt_bias[0]).reshape(1, 2 * HEADS), ((0, 0), (0, 128 - 2 * HEADS)))
    lbp = f32(lb_param).reshape(DEPTH + 1, 2 * QK)

    mod = _modulation(f32(c), f32(c_ctx), f32(w_mod[0]), f32(b_mod[0]))
    shift_l = mod[:bsz, None, 0:D_MODEL]
    scale_l = mod[:bsz, None, D_MODEL:2 * D_MODEL]
    gate_l = mod[:bsz, None, 2 * D_MODEL:]
    shift_c = mod[bsz:bsz + 1, None, 0:D_MODEL]
    scale_c = mod[bsz:bsz + 1, None, D_MODEL:2 * D_MODEL]

    lat = _pre_project(f32(x), shift_l, scale_l, w_pre, alog_vec, dtb_vec, lbp, cm=True)
    cx = _pre_project(f32(ctx).reshape(bsz * n_ctx, D_MODEL), shift_c, scale_c, w_pre, alog_vec, dtb_vec, lbp,
                      cm=False)
    aqkv_l = lat[0].reshape(bsz, length, 3 * QK)
    ab_l = lat[1].reshape(bsz, length, 128)
    gla_l = tuple(t.reshape(bsz, length, t.shape[-1]) for t in lat[2:])
    aqkv_c = cx[0].reshape(bsz, n_ctx, 3 * QK)
    ab_c = cx[1].reshape(bsz, n_ctx, 128)
    gla_c = tuple(t.reshape(bsz, n_ctx, t.shape[-1]) for t in cx[2:])

    oa = _gdn(aqkv_l, aqkv_c, ab_l, ab_c, f32(conv_w[0]))
    ob_cm = _gla(gla_l, gla_c)

    a_gain = jnp.tile(f32(a_norm_g[0]), HEADS).reshape(1, QK)
    b_gain = jnp.tile(f32(b_norm_g[0]), HEADS).reshape(1, QK)
    out = _post_project(f32(x), oa, ob_cm, shift_l, scale_l, gate_l, w_post,
                        f32(w_a_out[0]).astype(BF16), f32(w_b_out[0]).astype(BF16), f32(w_out[0]).astype(BF16),
                        a_gain, b_gain, f32(ln_g[0]).reshape(1, D_MODEL), f32(ln_b[0]).reshape(1, D_MODEL))
    return out.astype(x.dtype)
```

```python
import functools

import jax
import jax.numpy as jnp
from jax import lax
from jax.experimental import pallas as pl
from jax.experimental.pallas import tpu as pltpu

D_MODEL = 1024
GRID_W = 64
HEADS = 4
DH = 128
QK = HEADS * DH
CONV_K = 5
CHUNK = 64
SUPER = 256
HALF = 128
SUB = 32
HP = 2
GB = 2
DEPTH = 1
DEEPNORM_ALPHA = (2 * DEPTH) ** 0.25
LN_EPS = 1e-6
RMS_EPS = 1e-6
L2_EPS = 1e-6

LANES = 128
SUBLANES = 8
VMEM_BYTES = 64 * 1024 * 1024

TOK_TILE = 512
COLS_PER_TILE = TOK_TILE // GRID_W
PRE_PARTS = 4
POST_PARTS = 2
N_PRE = 3 * QK + LANES + 4 * QK
N_POST = 2 * QK + 2 * D_MODEL
VMEM_LIMIT = VMEM_BYTES - 6 * 1024 * 1024

F32 = jnp.float32
BF16 = jnp.bfloat16


def _dot(a, b):
    return jnp.dot(a.astype(BF16), b.astype(BF16), preferred_element_type=F32)


def _dot_nt(a, b):
    return lax.dot_general(a.astype(BF16), b.astype(BF16), (((1,), (1,)), ((), ())),
                           preferred_element_type=F32)


def _dot_exact(a, b):
    return jnp.dot(a, b, preferred_element_type=F32, precision=lax.Precision.HIGHEST)


def _sigmoid(x):
    return 1.0 / (1.0 + jnp.exp(-x))


def _sigmoid_pair(x):
    t = jnp.exp(-jnp.abs(x))
    r = 1.0 / (1.0 + t)
    tr = t * r
    pos = x >= 0
    return jnp.where(pos, r, tr), jnp.where(pos, tr, r)


def _silu(x):
    return x * _sigmoid(x)


def _layer_norm(t):
    mu = jnp.mean(t, axis=-1, keepdims=True)
    tc = t - mu
    var = jnp.mean(tc * tc, axis=-1, keepdims=True)
    return tc * lax.rsqrt(var + LN_EPS)


def _mod_kernel(c_ref, w_ref, b_ref, o_ref):
    o_ref[...] = _dot_exact(_silu(c_ref[...]), w_ref[...]) + b_ref[...]


def _modulation(c, c_ctx, w_mod, b_mod):
    bsz = c.shape[0]
    rows = -(-(bsz + 1) // SUBLANES) * SUBLANES
    cc = jnp.zeros((rows, D_MODEL), F32).at[:bsz].set(c).at[bsz].set(c_ctx)
    nblk = 3
    out = pl.pallas_call(
        _mod_kernel,
        grid=(nblk,),
        in_specs=[pl.BlockSpec((rows, D_MODEL), lambda j: (0, 0)),
                  pl.BlockSpec((D_MODEL, D_MODEL), lambda j: (0, j)),
                  pl.BlockSpec((1, D_MODEL), lambda j: (0, j))],
        out_specs=pl.BlockSpec((rows, D_MODEL), lambda j: (0, j)),
        out_shape=jax.ShapeDtypeStruct((rows, 3 * D_MODEL), F32),
        name="mod",
    )(cc, w_mod, b_mod.reshape(1, 3 * D_MODEL))
    return out


def _interleave_skewed(gens):
    live = list(enumerate(gens))
    rnd = 0
    while live:
        for i, gen in list(live):
            if rnd >= i:
                try:
                    next(gen)
                except StopIteration:
                    live.remove((i, gen))
        rnd += 1


def _load_raster(ref, grid_tile, part, parts):
    tok = TOK_TILE // parts
    if grid_tile:
        rows = GRID_W // parts
        return ref[0, part * rows:(part + 1) * rows].reshape(tok, ref.shape[-1])
    return ref[part * tok:(part + 1) * tok, :]


def _load_colmajor(ref, part, parts):
    cols = COLS_PER_TILE // parts
    return jnp.concatenate([ref[0, :, cl, :] for cl in range(part * cols, (part + 1) * cols)], axis=0)


def _store_raster(ref, val, grid_tile, part, parts, lanes=slice(None)):
    tok = TOK_TILE // parts
    if grid_tile:
        rows = GRID_W // parts
        ref[0, part * rows:(part + 1) * rows, :, lanes] = (
            val.reshape(rows, COLS_PER_TILE, val.shape[-1]).astype(ref.dtype))
    else:
        ref[part * tok:(part + 1) * tok, lanes] = val.astype(ref.dtype)


def _store_colmajor(ref, val, grid_tile, part, parts, lanes=slice(None)):
    tok = TOK_TILE // parts
    if grid_tile:
        cols = COLS_PER_TILE // parts
        for n, cl in enumerate(range(part * cols, (part + 1) * cols)):
            ref[0, cl, :, lanes] = val[n * GRID_W:(n + 1) * GRID_W, :].astype(ref.dtype)
    else:
        ref[part * tok:(part + 1) * tok, lanes] = val.astype(ref.dtype)


def _pre_kernel(x_ref, shift_ref, scale_ref, w_ref, alog_ref, dtb_ref, lbp_ref,
                aqkv_ref, ab_ref, bq_ref, bg_ref, bk_ref, bi_ref, *, cm):
    modulate = lambda t: (_layer_norm(t) * (1.0 + scale_ref[0]) + shift_ref[0]).astype(BF16)

    def part_stages(part):
        u = modulate(_load_raster(x_ref, cm, part, PRE_PARTS))
        yield
        for g in range(3):
            z = jnp.dot(u, w_ref[:, g * QK:(g + 1) * QK], preferred_element_type=F32)
            _store_raster(aqkv_ref, z, cm, part, PRE_PARTS, slice(g * QK, (g + 1) * QK))
            yield
        off = 3 * QK
        z = jnp.dot(u, w_ref[:, off:off + LANES], preferred_element_type=F32)
        zs = z + dtb_ref[...]
        softplus = jnp.maximum(zs, 0.0) + jnp.log(1.0 + jnp.exp(-jnp.abs(zs)))
        a_g = -jnp.exp(alog_ref[...]) * softplus
        a_b = _sigmoid(z)
        lane = lax.broadcasted_iota(jnp.int32, z.shape, 1)
        _store_raster(ab_ref, jnp.where(lane < 2 * HEADS, a_g, a_b), cm, part, PRE_PARTS)
        u_b = modulate(_load_colmajor(x_ref, part, PRE_PARTS)) if cm else u
        yield
        off += LANES
        z = jnp.dot(u_b, w_ref[:, off:off + QK], preferred_element_type=F32)
        _store_colmajor(bq_ref, _silu(z) * DH ** -0.5, cm, part, PRE_PARTS)
        yield
        off += QK
        p0 = lbp_ref[0:1, :]
        p1 = lbp_ref[1:2, :]
        pm = jnp.maximum(p0, p1)
        e0 = jnp.exp(p0 - pm)
        lb = e0 / (e0 + jnp.exp(p1 - pm))
        for d in range(2):
            z = jnp.dot(u_b, w_ref[:, off + d * QK: off + (d + 1) * QK], preferred_element_type=F32)
            s_pos, s_neg = _sigmoid_pair(z)
            lbd = lb[:, d * QK:(d + 1) * QK]
            lanes = slice(d * QK, (d + 1) * QK)
            _store_colmajor(bg_ref, jnp.log(lbd + (1.0 - lbd) * s_pos), cm, part, PRE_PARTS, lanes)
            _store_colmajor(bk_ref, (1.0 - lbd) * s_neg, cm, part, PRE_PARTS, lanes)
            yield
        off += 2 * QK
        z = jnp.dot(u_b, w_ref[:, off:off + QK], preferred_element_type=F32)
        _store_colmajor(bi_ref, z, cm, part, PRE_PARTS)

    _interleave_skewed([part_stages(p) for p in range(PRE_PARTS)])


def _pre_project(tokens, shift, scale, w_pre, alog_vec, dtb_vec, lbp, *, cm):
    consts = [pl.BlockSpec((D_MODEL, N_PRE), lambda *_: (0, 0)),
              pl.BlockSpec((1, LANES), lambda *_: (0, 0)),
              pl.BlockSpec((1, LANES), lambda *_: (0, 0)),
              pl.BlockSpec((2, 2 * QK), lambda *_: (0, 0))]
    if cm:
        bsz, length, _ = tokens.shape
        rows = length // GRID_W
        assert rows == GRID_W and GRID_W % COLS_PER_TILE == 0
        nj = GRID_W // COLS_PER_TILE
        x_view = tokens.reshape(bsz, rows, GRID_W, D_MODEL)
        grid = (bsz, nj)
        in_specs = [pl.BlockSpec((1, rows, COLS_PER_TILE, D_MODEL), lambda b, j: (b, 0, j, 0)),
                    pl.BlockSpec((1, 1, D_MODEL), lambda b, j: (b, 0, 0)),
                    pl.BlockSpec((1, 1, D_MODEL), lambda b, j: (b, 0, 0))] + consts

        def rast(width, dtype):
            return (jax.ShapeDtypeStruct((bsz, rows, GRID_W, width), dtype),
                    pl.BlockSpec((1, rows, COLS_PER_TILE, width), lambda b, j: (b, 0, j, 0)))

        def colm(width, dtype):
            return (jax.ShapeDtypeStruct((bsz, GRID_W, rows, width), dtype),
                    pl.BlockSpec((1, COLS_PER_TILE, rows, width), lambda b, j: (b, j, 0, 0)))
    else:
        n_tok = tokens.shape[0]
        assert n_tok % TOK_TILE == 0
        x_view = tokens
        grid = (n_tok // TOK_TILE,)
        in_specs = [pl.BlockSpec((TOK_TILE, D_MODEL), lambda i: (i, 0)),
                    pl.BlockSpec((1, 1, D_MODEL), lambda i: (0, 0, 0)),
                    pl.BlockSpec((1, 1, D_MODEL), lambda i: (0, 0, 0))] + consts

        def rast(width, dtype):
            return (jax.ShapeDtypeStruct((n_tok, width), dtype),
                    pl.BlockSpec((TOK_TILE, width), lambda i: (i, 0)))

        colm = rast

    outs = [rast(3 * QK, BF16), rast(LANES, F32), colm(QK, BF16), colm(2 * QK, F32),
            colm(2 * QK, BF16), colm(QK, BF16)]
    res = pl.pallas_call(
        functools.partial(_pre_kernel, cm=cm),
        grid=grid,
        in_specs=in_specs,
        out_specs=[o[1] for o in outs],
        out_shape=[o[0] for o in outs],
        compiler_params=pltpu.CompilerParams(vmem_limit_bytes=VMEM_LIMIT),
        name="pre_lat" if cm else "pre_ctx",
    )(x_view, shift, scale, w_pre, alog_vec, dtb_vec, lbp)
    return res


def _iota2(n):
    return (lax.broadcasted_iota(jnp.int32, (n, n), 0), lax.broadcasted_iota(jnp.int32, (n, n), 1))


def _scan_masks(rev):
    ii, jj = _iota2(SUPER)
    same = (ii ^ jj) < CHUNK
    incl = same & ((jj >= ii) if rev else (jj <= ii))
    strict = same & ((jj > ii) if rev else (jj < ii))
    return ii, jj, same, incl, strict


def _dot_split2(m01, g):
    m = m01.astype(BF16)
    g1 = g.astype(BF16)
    g2 = (g - g1.astype(F32)).astype(BF16)
    return jnp.dot(m, g1, preferred_element_type=F32) + jnp.dot(m, g2, preferred_element_type=F32)


def _chunk_rows(t, offset, span):
    return jnp.concatenate([jnp.broadcast_to(t[span * i + offset: span * i + offset + 1, :], (span, t.shape[1]))
                            for i in range(SUPER // span)], axis=0)


def _gdn_kernel(ql_ref, kl_ref, vl_ref, qc_ref, kc_ref, vc_ref, abl_ref, abc_ref,
                wq_ref, wk_ref, wv_ref, o_ref,
                q_s, k_s, v_s, wq_s, u_s, kdt_s, at_s, gl_s, st_s, *, n_ctx, n_lat):
    head0 = pl.program_id(1) * HP
    assert n_ctx == SUPER and n_lat % (2 * SUPER) == 0
    ns_lat = n_lat // SUPER
    cps = SUPER // CHUNK
    width = HP * DH

    halo = 2 * SUBLANES

    def conv_block(kind, src_ref, w_ref, dst_ref, src_r0, dst_r0, left_edge, right_edge):
        lo = 0 if left_edge else halo
        hi = 0 if right_edge else halo
        start = src_r0 - lo if isinstance(src_r0, int) else pl.multiple_of(src_r0 - lo, halo)
        parts = [src_ref[0, pl.ds(start, SUPER + lo + hi), :].astype(F32)]
        if left_edge:
            parts.insert(0, jnp.zeros((halo, width), F32))
        if right_edge:
            parts.append(jnp.zeros((halo, width), F32))
        xv = jnp.concatenate(parts, axis=0) if len(parts) > 1 else parts[0]
        acc = jnp.zeros((SUPER, width), F32)
        for j in range(CONV_K):
            sh = (CONV_K // 2 - j) % (SUPER + 2 * halo)
            tap = xv if sh == 0 else pltpu.roll(xv, sh, 0)
            acc = acc + tap[halo:halo + SUPER, :] * w_ref[j:j + 1, :]
        y = _silu(acc)
        for hh in range(HP):
            yh = y[:, hh * DH:(hh + 1) * DH]
            if kind != "v":
                yh = yh * lax.rsqrt(jnp.sum(yh * yh, axis=-1, keepdims=True) + L2_EPS)
            if kind == "q":
                yh = yh * DH ** -0.5
            dst_ref[pl.ds(dst_r0, SUPER), hh * DH:(hh + 1) * DH] = yh.astype(dst_ref.dtype)

    streams = (("q", ql_ref, qc_ref, wq_ref, q_s), ("k", kl_ref, kc_ref, wk_ref, k_s),
               ("v", vl_ref, vc_ref, wv_ref, v_s))

    def conv(blocks):
        for sc, left_edge, right_edge in blocks:
            for kind, lat_ref, ctx_ref, w_ref, dst_ref in streams:
                if isinstance(sc, int) and sc == 0:
                    conv_block(kind, ctx_ref, w_ref, dst_ref, 0, 0, True, True)
                else:
                    conv_block(kind, lat_ref, w_ref, dst_ref, (sc - 1) * SUPER,
                               sc * SUPER if isinstance(sc, int) else pl.multiple_of(sc * SUPER, SUPER),
                               left_edge, right_edge)
                yield

    lane = lax.broadcasted_iota(jnp.int32, (SUPER, LANES), 1)

    def prep(items, slot):
        idx = range(len(items))
        masks = {d: _scan_masks(d == 1) for d in {it[1] for it in items}}
        incl = [masks[it[1]][3] for it in items]
        strict = [masks[it[1]][4] for it in items]
        revs = [it[1] == 1 for it in items]
        r0 = [pl.multiple_of(it[3] * SUPER, SUPER) for it in items]
        hs = [slice(it[0] * DH, (it[0] + 1) * DH) for it in items]
        q = [q_s[pl.ds(r0[i], SUPER), hs[i]].astype(F32) for i in idx]
        k = [k_s[pl.ds(r0[i], SUPER), hs[i]].astype(F32) for i in idx]
        v = [v_s[pl.ds(r0[i], SUPER), hs[i]].astype(F32) for i in idx]
        kq = [_dot_nt(jnp.concatenate([k[i], q[i]], axis=0), k[i]) for i in idx]
        cols = [it[1] * HEADS + head0 + it[0] for it in items]
        pick = lambda ab, col: jnp.broadcast_to(
            jnp.sum(jnp.where(lane == col, ab, 0.0), axis=-1, keepdims=True), (SUPER, DH))
        beta = [pick(items[i][2], 2 * HEADS + cols[i]) for i in idx]
        lower = jnp.where(masks[0][3] if 0 in masks else _scan_masks(False)[3], 1.0, 0.0)
        prefix = {}
        for it in items:
            if id(it[2]) not in prefix:
                prefix[id(it[2])] = _dot_split2(lower, it[2])
        gcum = []
        for i in idx:
            pre = pick(prefix[id(items[i][2])], cols[i])
            if revs[i]:
                pre = _chunk_rows(pre, CHUNK - 1, CHUNK) - pre + pick(items[i][2], cols[i])
            gcum.append(pre)
        yield
        gtot = [_chunk_rows(gcum[i], 0 if revs[i] else CHUNK - 1, CHUNK) for i in idx]
        grow = [jnp.transpose(gcum[i])[0:1, :] for i in idx]
        decay = [jnp.where(incl[i], jnp.exp(jnp.where(
            incl[i], jnp.concatenate([gcum[i], gcum[i]], axis=1) - grow[i], 0.0)), 0.0) for i in idx]
        e_g = [jnp.exp(gcum[i]) for i in idx]
        a_mat = [jnp.where(strict[i], kq[i][:SUPER] * jnp.concatenate([beta[i], beta[i]], axis=1) * decay[i], 0.0)
                 for i in idx]
        attn = [kq[i][SUPER:] * decay[i] for i in idx]
        for c in range(cps):
            rs = slice(c * CHUNK, (c + 1) * CHUNK)
            for i in idx:
                at_s[slot, items[i][0], items[i][1], items[i][4], rs, :] = attn[i][rs, rs].astype(BF16)
        halves = [(i, h) for i in idx for h in range(SUPER // HALF)]
        a_half = [a_mat[i][h * HALF:(h + 1) * HALF, h * HALF:(h + 1) * HALF] for i, h in halves]
        ii, jj = _iota2(HALF)
        eye = jnp.where(ii == jj, 1.0, 0.0).astype(F32)
        t = None
        s = 1
        while s < CHUNK:
            a_off = []
            for n, (i, _) in enumerate(halves):
                row, col = (jj, ii) if revs[i] else (ii, jj)
                m = ((row ^ col) < 2 * s) & ((row & s) != 0) & ((col & s) == 0)
                a_off.append(jnp.where(m, a_half[n], 0.0))
            if t is None:
                t = [eye - a for a in a_off]
            else:
                p = [_dot(a_off[n], t[n]) for n in range(len(halves))]
                yield
                t = [t[n] - _dot(t[n], p[n]) for n in range(len(halves))]
                yield
            s *= 2
        rhs = [jnp.concatenate([k[i] * beta[i] * e_g[i], v[i] * beta[i]], axis=1) for i in idx]
        sol_half = [_dot(t[n], rhs[i][h * HALF:(h + 1) * HALF]) for n, (i, h) in enumerate(halves)]
        sol = [jnp.concatenate([sol_half[n] for n, (j, _) in enumerate(halves) if j == i], axis=0)
               for i in idx]
        qg = [q[i] * e_g[i] for i in idx]
        kdt = [jnp.transpose(k[i] * jnp.exp(gtot[i] - gcum[i])) for i in idx]
        yield
        for i in idx:
            hh, d, e = items[i][0], items[i][1], items[i][4]
            for c in range(cps):
                rs = slice(c * CHUNK, (c + 1) * CHUNK)
                wq_s[slot, hh, d, e, c] = jnp.concatenate([sol[i][rs, :DH], qg[i][rs, :]], axis=0).astype(BF16)
            u_s[slot, hh, d, e] = sol[i][:, DH:]
            kdt_s[slot, hh, d, e] = kdt[i].astype(BF16)
            gl_s[slot, hh, d, e] = jnp.concatenate(
                [jnp.exp(gtot[i][c * CHUNK:c * CHUNK + 1, :]) for c in range(cps)]
                + [jnp.zeros((SUBLANES - cps, DH), F32)], axis=0)

    def scan(chains, slot, mode):
        idx = range(len(chains))
        st = [st_s[ch[0], ch[1]] for ch in chains]
        for e in range(len(chains[0][2])):
            gl = [gl_s[slot, ch[0], ch[1], e] for ch in chains]
            for step_c in range(cps):
                cs = [(cps - 1 - step_c) if ch[1] == 1 else step_c for ch in chains]
                rs = [slice(c * CHUNK, (c + 1) * CHUNK) for c in cs]
                ws = [jnp.dot(wq_s[slot, chains[i][0], chains[i][1], e, cs[i]], st[i].astype(BF16),
                              preferred_element_type=F32) for i in idx]
                yield
                v_new = [(u_s[slot, chains[i][0], chains[i][1], e, rs[i], :] - ws[i][:CHUNK]).astype(BF16)
                         for i in idx]
                st = [st[i] * gl[i][cs[i]:cs[i] + 1, :]
                      + jnp.dot(kdt_s[slot, chains[i][0], chains[i][1], e, :, rs[i]], v_new[i],
                                preferred_element_type=F32) for i in idx]
                if mode is not None:
                    for i in idx:
                        hh, d, scs = chains[i]
                        o = ws[i][CHUNK:] + jnp.dot(at_s[slot, hh, d, e, rs[i], :], v_new[i],
                                                    preferred_element_type=F32)
                        ro = pl.multiple_of(scs[e] * SUPER - n_ctx + cs[i] * CHUNK, CHUNK)
                        if mode == "set":
                            o_ref[0, pl.ds(ro, CHUNK), hh * DH:(hh + 1) * DH] = o
                        else:
                            o_ref[0, pl.ds(ro, CHUNK), hh * DH:(hh + 1) * DH] += o
                yield
        for i in idx:
            st_s[chains[i][0], chains[i][1]] = st[i]

    def interleave(*gens):
        live = list(gens)
        while live:
            for gen in list(live):
                try:
                    next(gen)
                except StopIteration:
                    live.remove(gen)

    def lat_ab(sc):
        return abl_ref[0, pl.ds(pl.multiple_of((sc - 1) * SUPER, SUPER), SUPER), :]

    def group(u, d):
        if isinstance(u, int) and u == 0:
            return [0]
        first = GB * (u - 1)
        return [first + 1 + e if d == 0 else ns_lat - first - e for e in range(GB)]

    def step(u, mode, with_prep, with_conv):
        slot = u % 2
        gens = [scan([(hh, d, group(u, d)) for hh in range(HP) for d in range(2)], slot, mode)]
        if with_prep:
            nxt = [[(sc, lat_ab(sc)) for sc in group(u + 1, d)] for d in range(2)]
            gens.append(prep([(hh, d, nxt[d][e][1], nxt[d][e][0], e)
                              for hh in range(HP) for d in range(2) for e in range(GB)], 1 - slot))
        if with_conv:
            gens.append(conv([(sc, False, False) for d in range(2) for sc in group(u + 2, d)]))
        interleave(*gens)

    assert ns_lat % (2 * GB) == 0 and ns_lat >= 4 * GB
    n_groups = ns_lat // GB
    conv_steps = (ns_lat - 4 * GB) // (2 * GB) + 1
    st_s[...] = jnp.zeros_like(st_s)
    interleave(conv([(0, True, True)]))
    ab_c = abc_ref[0]
    edge = lambda sc: (sc, sc == 1, sc == ns_lat)
    interleave(prep([(hh, d, ab_c, 0, 0) for hh in range(HP) for d in range(2)], 0),
               conv([edge(sc) for d in range(2) for sc in group(1, d)]))
    step(0, None, True, True)

    def run(lo, hi, mode, with_conv):
        def body(u, _):
            step(u, mode, True, with_conv)
            return 0
        lax.fori_loop(lo, hi, body, 0)

    run(1, conv_steps, "set", True)
    run(conv_steps, n_groups // 2 + 1, "set", False)
    run(n_groups // 2 + 1, n_groups, "add", False)
    step(n_groups, "add", False, False)


def _gdn(aqkv_lat, aqkv_ctx, ab_lat, ab_ctx, conv_w):
    bsz, n_lat, _ = aqkv_lat.shape
    n_ctx = aqkv_ctx.shape[1]
    n_all = n_ctx + n_lat
    width = HP * DH
    groups = HEADS // HP
    cps = SUPER // CHUNK

    def stream(n, j0):
        return pl.BlockSpec((1, n, width), lambda b, p: (b, 0, j0 + p))

    def wspec(j0):
        return pl.BlockSpec((CONV_K, width), lambda b, p: (0, j0 + p))

    return pl.pallas_call(
        functools.partial(_gdn_kernel, n_ctx=n_ctx, n_lat=n_lat),
        grid=(bsz, groups),
        in_specs=[stream(n_lat, 0), stream(n_lat, groups), stream(n_lat, 2 * groups),
                  stream(n_ctx, 0), stream(n_ctx, groups), stream(n_ctx, 2 * groups),
                  pl.BlockSpec((1, n_lat, LANES), lambda b, p: (b, 0, 0)),
                  pl.BlockSpec((1, n_ctx, LANES), lambda b, p: (b, 0, 0)),
                  wspec(0), wspec(groups), wspec(2 * groups)],
        out_specs=pl.BlockSpec((1, n_lat, width), lambda b, p: (b, 0, p)),
        out_shape=jax.ShapeDtypeStruct((bsz, n_lat, QK), F32),
        scratch_shapes=[pltpu.VMEM((n_all, width), BF16), pltpu.VMEM((n_all, width), BF16),
                        pltpu.VMEM((n_all, width), BF16),
                        pltpu.VMEM((2, HP, 2, GB, cps, 2 * CHUNK, DH), BF16),
                        pltpu.VMEM((2, HP, 2, GB, SUPER, DH), F32),
                        pltpu.VMEM((2, HP, 2, GB, DH, SUPER), BF16),
                        pltpu.VMEM((2, HP, 2, GB, SUPER, CHUNK), BF16),
                        pltpu.VMEM((2, HP, 2, GB, SUBLANES, DH), F32),
                        pltpu.VMEM((HP, 2, DH, DH), F32)],
        compiler_params=pltpu.CompilerParams(vmem_limit_bytes=VMEM_LIMIT),
        name="gdn",
    )(aqkv_lat, aqkv_lat, aqkv_lat, aqkv_ctx, aqkv_ctx, aqkv_ctx, ab_lat, ab_ctx,
      conv_w, conv_w, conv_w)


def _gla_kernel(ql_ref, gfl_ref, gbl_ref, kfl_ref, kbl_ref, vl_ref,
                qc_ref, gfc_ref, gbc_ref, kfc_ref, kbc_ref, vc_ref,
                o_ref, st_s, *, n_ctx, n_lat):
    assert n_ctx == SUPER and n_lat % (2 * SUPER) == 0
    ns_lat = n_lat // SUPER
    cps = SUPER // CHUNK

    def scan_supers(items, mode):
        idx = range(len(items))
        revs = [it[2] == 1 for it in items]
        masks = {d: _scan_masks(d == 1) for d in {it[2] for it in items}}
        r0 = [pl.multiple_of(it[3] * SUPER, SUPER) for it in items]
        hs = [slice(it[1] * DH, (it[1] + 1) * DH) for it in items]
        g = [items[i][0][1][0, pl.ds(r0[i], SUPER), hs[i]] for i in idx]
        k = [items[i][0][2][0, pl.ds(r0[i], SUPER), hs[i]].astype(F32) for i in idx]
        v = [items[i][0][3][0, pl.ds(r0[i], SUPER), hs[i]] for i in idx]
        lower = jnp.where(_scan_masks(False)[3], 1.0, 0.0)
        pre = {hh: _dot_split2(lower, jnp.concatenate([g[i] for i in idx if items[i][1] == hh], axis=1))
               for hh in sorted({it[1] for it in items})}
        yield
        gcum = [None] * len(items)
        for hh, pr in pre.items():
            for n, i in enumerate([i for i in idx if items[i][1] == hh]):
                p = pr[:, n * DH:(n + 1) * DH]
                gcum[i] = (_chunk_rows(p, CHUNK - 1, CHUNK) - p + g[i]) if revs[i] else p
        g_mid = [_chunk_rows(gcum[i], SUB // 2 if revs[i] else SUB // 2 - 1, SUB) for i in idx]
        g_tot = [_chunk_rows(gcum[i], 0 if revs[i] else CHUNK - 1, CHUNK) for i in idx]
        kd = [k[i] * jnp.exp(g_mid[i] - gcum[i]) for i in idx]
        k_out = [kd[i] * jnp.exp(g_tot[i] - g_mid[i]) for i in idx]
        if mode is not None:
            q = [items[i][0][0][0, pl.ds(r0[i], SUPER), hs[i]].astype(F32) for i in idx]
            row = lax.broadcasted_iota(jnp.int32, (SUPER, DH), 0) & (CHUNK - 1)
            first = [(row >= SUB) if revs[i] else (row < SUB) for i in idx]
            g_bnd = [_chunk_rows(gcum[i], SUB if revs[i] else SUB - 1, CHUNK) for i in idx]
            qd = [q[i] * jnp.exp(gcum[i] - g_mid[i]) for i in idx]
            e_b = [jnp.exp(jnp.where(first[i], g_bnd[i] - g_mid[i], g_mid[i] - g_bnd[i])) for i in idx]
            qo = [jnp.where(first[i], 0.0, qd[i] * e_b[i]) for i in idx]
            ko = [jnp.where(first[i], kd[i] * e_b[i], 0.0) for i in idx]
            sc1 = [_dot_nt(qd[i], kd[i]) for i in idx]
            sc2 = [_dot_nt(qo[i], ko[i]) for i in idx]
            q_in = [qd[i] * jnp.exp(g_mid[i]) for i in idx]
            yield
            attn = []
            for i in idx:
                ii, jj, same, incl, _ = masks[items[i][2]]
                attn.append(jnp.where(incl & ((ii ^ jj) < SUB), sc1[i], 0.0) + jnp.where(same, sc2[i], 0.0))
            o_intra = [_dot(attn[i], v[i]) for i in idx]
        d_st = [[_dot(jnp.transpose(v[i][c * CHUNK:(c + 1) * CHUNK].astype(F32)), k_out[i][c * CHUNK:(c + 1) * CHUNK])
                 for c in range(cps)] for i in idx]
        yield
        st = [st_s[it[1], it[2]] for it in items]
        st_in = [[None] * cps for _ in idx]
        for step_c in range(cps):
            for i in idx:
                c = (cps - 1 - step_c) if revs[i] else step_c
                st_in[i][c] = st[i]
                st[i] = st[i] * jnp.exp(g_tot[i][c * CHUNK:c * CHUNK + 1, :]) + d_st[i][c]
        outs = None
        if mode is not None:
            outs = [[o_intra[i][c * CHUNK:(c + 1) * CHUNK] + _dot_nt(q_in[i][c * CHUNK:(c + 1) * CHUNK], st_in[i][c])
                     for c in range(cps)] for i in idx]
        for i in idx:
            _, hh, d, _ = items[i]
            st_s[hh, d] = st[i]
            if mode == "set":
                o_ref[0, pl.ds(r0[i], SUPER), hs[i]] = jnp.concatenate(outs[i], axis=0)
            elif mode == "add":
                o_ref[0, pl.ds(r0[i], SUPER), hs[i]] += jnp.concatenate(outs[i], axis=0)

    def interleave(*gens):
        live = list(gens)
        while live:
            for gen in list(live):
                try:
                    next(gen)
                except StopIteration:
                    live.remove(gen)

    lat = ((ql_ref, gfl_ref, kfl_ref, vl_ref), (ql_ref, gbl_ref, kbl_ref, vl_ref))
    ctx = ((qc_ref, gfc_ref, kfc_ref, vc_ref), (qc_ref, gbc_ref, kbc_ref, vc_ref))
    st_s[...] = jnp.zeros_like(st_s)
    interleave(scan_supers([(ctx[d], hh, d, 0) for hh in range(HP) for d in range(2)], None))

    def run(lo, hi, mode):
        def body(n, _):
            interleave(*[scan_supers([(lat[d], hh, d, (2 * n + e) if d == 0 else (ns_lat - 1 - 2 * n - e))
                                      for hh in range(HP) for d in range(2)], mode) for e in range(2)])
            return 0
        lax.fori_loop(lo, hi, body, 0)

    assert ns_lat % 4 == 0
    run(0, ns_lat // 4, "set")
    run(ns_lat // 4, ns_lat // 2, "add")


def _gla(lat, ctx):
    bsz, n_lat, _ = lat[0].shape
    n_ctx = ctx[0].shape[1]
    width = HP * DH
    groups = HEADS // HP

    def specs(n):
        one = lambda j0: pl.BlockSpec((1, n, width), lambda b, p: (b, 0, j0 + p))
        return [one(0), one(0), one(groups), one(0), one(groups), one(0)]

    def args(t):
        q, g, k, v = t
        return [q, g, g, k, k, v]

    return pl.pallas_call(
        functools.partial(_gla_kernel, n_ctx=n_ctx, n_lat=n_lat),
        grid=(bsz, groups),
        in_specs=specs(n_lat) + specs(n_ctx),
        out_specs=pl.BlockSpec((1, n_lat, width), lambda b, p: (b, 0, p)),
        out_shape=jax.ShapeDtypeStruct((bsz, n_lat, QK), F32),
        scratch_shapes=[pltpu.VMEM((HP, 2, DH, DH), F32)],
        compiler_params=pltpu.CompilerParams(vmem_limit_bytes=VMEM_LIMIT),
        name="gla",
    )(*args(lat), *args(ctx))


def _gated_rms(o, gain, gate):
    parts = []
    for hh in range(HEADS):
        oh = o[:, hh * DH:(hh + 1) * DH]
        ms = jnp.mean(oh * oh, axis=-1, keepdims=True)
        parts.append(oh * lax.rsqrt(ms + RMS_EPS))
    return jnp.concatenate(parts, axis=1) * gain * _silu(gate)


def _post_kernel(x_ref, oa_ref, ob_ref, shift_ref, scale_ref, gate_ref, w_ref, wa_ref, wb_ref, wo_ref,
                 ga_ref, gb_ref, lng_ref, lnb_ref, out_ref, ob_s):
    for cl in range(COLS_PER_TILE):
        ob_s[:, cl, :] = ob_ref[0, cl]

    def part_stages(part):
        rows = GRID_W // POST_PARTS
        xt = _load_raster(x_ref, True, part, POST_PARTS)
        u = (_layer_norm(xt) * (1.0 + scale_ref[0]) + shift_ref[0]).astype(BF16)
        yield
        za = jnp.dot(u, w_ref[:, 0:QK], preferred_element_type=F32)
        y_a = _dot(_gated_rms(_load_raster(oa_ref, True, part, POST_PARTS), ga_ref[...], za), wa_ref[...])
        yield
        zb = jnp.dot(u, w_ref[:, QK:2 * QK], preferred_element_type=F32)
        ob = ob_s[part * rows:(part + 1) * rows].reshape(TOK_TILE // POST_PARTS, QK)
        y_b = _dot(_gated_rms(ob, gb_ref[...], zb), wb_ref[...])
        yield
        mix = _sigmoid(jnp.dot(u, w_ref[:, 2 * QK:2 * QK + D_MODEL], preferred_element_type=F32)) * y_a
        yield
        mix = mix + _sigmoid(jnp.dot(u, w_ref[:, 2 * QK + D_MODEL:], preferred_element_type=F32)) * y_b
        yield
        sub = _dot(mix, wo_ref[...])
        hres = DEEPNORM_ALPHA * xt + gate_ref[0] * sub
        y = _layer_norm(hres) * lng_ref[...] + lnb_ref[...]
        _store_raster(out_ref, y, True, part, POST_PARTS)

    _interleave_skewed([part_stages(p) for p in range(POST_PARTS)])


def _post_project(x, oa, ob_cm, shift, scale, gate, w_post, w_a_out, w_b_out, w_out, a_gain, b_gain, ln_g, ln_b):
    bsz, length, _ = x.shape
    rows = length // GRID_W
    nj = GRID_W // COLS_PER_TILE
    const = lambda shape: pl.BlockSpec(shape, lambda b, j: tuple(0 for _ in shape))
    rast = lambda width: pl.BlockSpec((1, rows, COLS_PER_TILE, width), lambda b, j: (b, 0, j, 0))
    modv = pl.BlockSpec((1, 1, D_MODEL), lambda b, j: (b, 0, 0))
    out = pl.pallas_call(
        _post_kernel,
        grid=(bsz, nj),
        in_specs=[rast(D_MODEL), rast(QK),
                  pl.BlockSpec((1, COLS_PER_TILE, rows, QK), lambda b, j: (b, j, 0, 0)),
                  modv, modv, modv,
                  const((D_MODEL, N_POST)), const((QK, D_MODEL)), const((QK, D_MODEL)),
                  const((D_MODEL, D_MODEL)), const((1, QK)), const((1, QK)),
                  const((1, D_MODEL)), const((1, D_MODEL))],
        out_specs=rast(D_MODEL),
        out_shape=jax.ShapeDtypeStruct((bsz, rows, GRID_W, D_MODEL), F32),
        scratch_shapes=[pltpu.VMEM((rows, COLS_PER_TILE, QK), F32)],
        compiler_params=pltpu.CompilerParams(vmem_limit_bytes=VMEM_LIMIT),
        name="post",
    )(x.reshape(bsz, rows, GRID_W, D_MODEL), oa.reshape(bsz, rows, GRID_W, QK),
      ob_cm.reshape(bsz, GRID_W, rows, QK), shift, scale, gate,
      w_post, w_a_out, w_b_out, w_out, a_gain, b_gain, ln_g, ln_b)
    return out.reshape(bsz, length, D_MODEL)


def kernel(x, c, ctx, c_ctx, w_mod, b_mod, w_in, conv_w, a_log, dt_bias, lb_param, a_norm_g, b_norm_g,
           w_a_out, w_b_out, w_out, ln_g, ln_b):
    assert w_mod.shape[0] == DEPTH
    bsz, length, _ = x.shape
    n_ctx = ctx.shape[1]
    f32 = lambda t: t.astype(F32)

    w = f32(w_in[0])
    o_alpha = 3 * QK
    o_agate = o_alpha + 4 * HEADS
    o_bq = o_agate + QK
    o_bgate = o_bq + 4 * QK
    w_pre = jnp.concatenate([w[:, :o_alpha], jnp.pad(w[:, o_alpha:o_agate], ((0, 0), (0, LANES - 4 * HEADS))),
                             w[:, o_bq:o_bgate]], axis=1).astype(BF16)
    w_post = jnp.concatenate([w[:, o_agate:o_bq], w[:, o_bgate:]], axis=1).astype(BF16)
    alog_vec = jnp.pad(f32(a_log[0]).reshape(1, 2 * HEADS), ((0, 0), (0, LANES - 2 * HEADS)))
    dtb_vec = jnp.pad(f32(dt_bias[0]).reshape(1, 2 * HEADS), ((0, 0), (0, LANES - 2 * HEADS)))
    lbp = f32(lb_param).reshape(DEPTH + 1, 2 * QK)

    mod = _modulation(f32(c), f32(c_ctx), f32(w_mod[0]), f32(b_mod[0]))
    shift_l = mod[:bsz, None, 0:D_MODEL]
    scale_l = mod[:bsz, None, D_MODEL:2 * D_MODEL]
    gate_l = mod[:bsz, None, 2 * D_MODEL:]
    shift_c = mod[bsz:bsz + 1, None, 0:D_MODEL]
    scale_c = mod[bsz:bsz + 1, None, D_MODEL:2 * D_MODEL]

    lat = _pre_project(f32(x), shift_l, scale_l, w_pre, alog_vec, dtb_vec, lbp, cm=True)
    cx = _pre_project(f32(ctx).reshape(bsz * n_ctx, D_MODEL), shift_c, scale_c, w_pre, alog_vec, dtb_vec, lbp,
                      cm=False)
    aqkv_l = lat[0].reshape(bsz, length, 3 * QK)
    ab_l = lat[1].reshape(bsz, length, LANES)
    gla_l = tuple(t.reshape(bsz, length, t.shape[-1]) for t in lat[2:])
    aqkv_c = cx[0].reshape(bsz, n_ctx, 3 * QK)
    ab_c = cx[1].reshape(bsz, n_ctx, LANES)
    gla_c = tuple(t.reshape(bsz, n_ctx, t.shape[-1]) for t in cx[2:])

    oa = _gdn(aqkv_l, aqkv_c, ab_l, ab_c, f32(conv_w[0]))
    ob_cm = _gla(gla_l, gla_c)

    a_gain = jnp.tile(f32(a_norm_g[0]), HEADS).reshape(1, QK)
    b_gain = jnp.tile(f32(b_norm_g[0]), HEADS).reshape(1, QK)
    out = _post_project(f32(x), oa, ob_cm, shift_l, scale_l, gate_l, w_post,
                        f32(w_a_out[0]).astype(BF16), f32(w_b_out[0]).astype(BF16), f32(w_out[0]).astype(BF16),
                        a_gain, b_gain, f32(ln_g[0]).reshape(1, D_MODEL), f32(ln_b[0]).reshape(1, D_MODEL))
    return out.astype(x.dtype)
```

```python
import functools

import jax
import jax.numpy as jnp
from jax import lax
from jax.experimental import pallas as pl
from jax.experimental.pallas import tpu as pltpu

D_MODEL = 1024
GRID_W = 64
HEADS = 4
DH = 128
QK = HEADS * DH
CONV_K = 5
CHUNK = 64
SUPER = 256
HALF = 128
SUB = 32
HP = 2
GB = 2
DEPTH = 1
DEEPNORM_ALPHA = (2 * DEPTH) ** 0.25
LN_EPS = 1e-6
RMS_EPS = 1e-6
L2_EPS = 1e-6

LANES = 128
SUBLANES = 8
VMEM_BYTES = 64 * 1024 * 1024

TOK_TILE = 1024
COLS_PER_TILE = TOK_TILE // GRID_W
PRE_PARTS = 8
POST_PARTS = 2
N_PRE = 3 * QK + LANES + 4 * QK
N_POST = 2 * QK + 2 * D_MODEL
VMEM_LIMIT = VMEM_BYTES - 6 * 1024 * 1024

F32 = jnp.float32
BF16 = jnp.bfloat16


def _dot(a, b):
    return jnp.dot(a.astype(BF16), b.astype(BF16), preferred_element_type=F32)


def _dot_nt(a, b):
    return lax.dot_general(a.astype(BF16), b.astype(BF16), (((1,), (1,)), ((), ())),
                           preferred_element_type=F32)


def _dot_exact(a, b):
    return jnp.dot(a, b, preferred_element_type=F32, precision=lax.Precision.HIGHEST)


def _sigmoid(x):
    return 1.0 / (1.0 + jnp.exp(-x))


def _sigmoid_pair(x):
    t = jnp.exp(-jnp.abs(x))
    r = 1.0 / (1.0 + t)
    tr = t * r
    pos = x >= 0
    return jnp.where(pos, r, tr), jnp.where(pos, tr, r)


def _silu(x):
    return x * _sigmoid(x)


def _layer_norm(t):
    mu = jnp.mean(t, axis=-1, keepdims=True)
    tc = t - mu
    var = jnp.mean(tc * tc, axis=-1, keepdims=True)
    return tc * lax.rsqrt(var + LN_EPS)


def _mod_kernel(c_ref, w_ref, b_ref, o_ref):
    o_ref[...] = _dot_exact(_silu(c_ref[...]), w_ref[...]) + b_ref[...]


def _modulation(c, c_ctx, w_mod, b_mod):
    bsz = c.shape[0]
    rows = -(-(bsz + 1) // SUBLANES) * SUBLANES
    cc = jnp.zeros((rows, D_MODEL), F32).at[:bsz].set(c).at[bsz].set(c_ctx)
    nblk = 3
    out = pl.pallas_call(
        _mod_kernel,
        grid=(nblk,),
        in_specs=[pl.BlockSpec((rows, D_MODEL), lambda j: (0, 0)),
                  pl.BlockSpec((D_MODEL, D_MODEL), lambda j: (0, j)),
                  pl.BlockSpec((1, D_MODEL), lambda j: (0, j))],
        out_specs=pl.BlockSpec((rows, D_MODEL), lambda j: (0, j)),
        out_shape=jax.ShapeDtypeStruct((rows, 3 * D_MODEL), F32),
        name="mod",
    )(cc, w_mod, b_mod.reshape(1, 3 * D_MODEL))
    return out


def _interleave_skewed(gens):
    live = list(enumerate(gens))
    rnd = 0
    while live:
        for i, gen in list(live):
            if rnd >= i:
                try:
                    next(gen)
                except StopIteration:
                    live.remove((i, gen))
        rnd += 1


def _load_raster(ref, grid_tile, part, parts):
    tok = TOK_TILE // parts
    if grid_tile:
        rows = GRID_W // parts
        return ref[0, part * rows:(part + 1) * rows].reshape(tok, ref.shape[-1])
    return ref[part * tok:(part + 1) * tok, :]


def _load_colmajor(ref, part, parts):
    cols = COLS_PER_TILE // parts
    return jnp.concatenate([ref[0, :, cl, :] for cl in range(part * cols, (part + 1) * cols)], axis=0)


def _store_raster(ref, val, grid_tile, part, parts, lanes=slice(None)):
    tok = TOK_TILE // parts
    if grid_tile:
        rows = GRID_W // parts
        ref[0, part * rows:(part + 1) * rows, :, lanes] = (
            val.reshape(rows, COLS_PER_TILE, val.shape[-1]).astype(ref.dtype))
    else:
        ref[part * tok:(part + 1) * tok, lanes] = val.astype(ref.dtype)


def _store_colmajor(ref, val, grid_tile, part, parts, lanes=slice(None)):
    tok = TOK_TILE // parts
    if grid_tile:
        cols = COLS_PER_TILE // parts
        for n, cl in enumerate(range(part * cols, (part + 1) * cols)):
            ref[0, cl, :, lanes] = val[n * GRID_W:(n + 1) * GRID_W, :].astype(ref.dtype)
    else:
        ref[part * tok:(part + 1) * tok, lanes] = val.astype(ref.dtype)


def _pre_kernel(x_ref, shift_ref, scale_ref, w_ref, alog_ref, dtb_ref, lbp_ref,
                aqkv_ref, ab_ref, bq_ref, bg_ref, bk_ref, bi_ref, *, cm):
    modulate = lambda t: (_layer_norm(t) * (1.0 + scale_ref[0]) + shift_ref[0]).astype(BF16)

    def part_stages(part):
        u = modulate(_load_raster(x_ref, cm, part, PRE_PARTS))
        yield
        for g in range(3):
            z = jnp.dot(u, w_ref[:, g * QK:(g + 1) * QK], preferred_element_type=F32)
            _store_raster(aqkv_ref, z, cm, part, PRE_PARTS, slice(g * QK, (g + 1) * QK))
            yield
        off = 3 * QK
        z = jnp.dot(u, w_ref[:, off:off + LANES], preferred_element_type=F32)
        zs = z + dtb_ref[...]
        softplus = jnp.maximum(zs, 0.0) + jnp.log(1.0 + jnp.exp(-jnp.abs(zs)))
        a_g = -jnp.exp(alog_ref[...]) * softplus
        a_b = _sigmoid(z)
        lane = lax.broadcasted_iota(jnp.int32, z.shape, 1)
        _store_raster(ab_ref, jnp.where(lane < 2 * HEADS, a_g, a_b), cm, part, PRE_PARTS)
        u_b = modulate(_load_colmajor(x_ref, part, PRE_PARTS)) if cm else u
        yield
        off += LANES
        z = jnp.dot(u_b, w_ref[:, off:off + QK], preferred_element_type=F32)
        _store_colmajor(bq_ref, _silu(z) * DH ** -0.5, cm, part, PRE_PARTS)
        yield
        off += QK
        p0 = lbp_ref[0:1, :]
        p1 = lbp_ref[1:2, :]
        pm = jnp.maximum(p0, p1)
        e0 = jnp.exp(p0 - pm)
        lb = e0 / (e0 + jnp.exp(p1 - pm))
        for d in range(2):
            z = jnp.dot(u_b, w_ref[:, off + d * QK: off + (d + 1) * QK], preferred_element_type=F32)
            s_pos, s_neg = _sigmoid_pair(z)
            lbd = lb[:, d * QK:(d + 1) * QK]
            lanes = slice(d * QK, (d + 1) * QK)
            _store_colmajor(bg_ref, jnp.log(lbd + (1.0 - lbd) * s_pos), cm, part, PRE_PARTS, lanes)
            _store_colmajor(bk_ref, (1.0 - lbd) * s_neg, cm, part, PRE_PARTS, lanes)
            yield
        off += 2 * QK
        z = jnp.dot(u_b, w_ref[:, off:off + QK], preferred_element_type=F32)
        _store_colmajor(bi_ref, z, cm, part, PRE_PARTS)

    _interleave_skewed([part_stages(p) for p in range(PRE_PARTS)])


def _pre_project(tokens, shift, scale, w_pre, alog_vec, dtb_vec, lbp, *, cm):
    consts = [pl.BlockSpec((D_MODEL, N_PRE), lambda *_: (0, 0)),
              pl.BlockSpec((1, LANES), lambda *_: (0, 0)),
              pl.BlockSpec((1, LANES), lambda *_: (0, 0)),
              pl.BlockSpec((2, 2 * QK), lambda *_: (0, 0))]
    if cm:
        bsz, length, _ = tokens.shape
        rows = length // GRID_W
        assert rows == GRID_W and GRID_W % COLS_PER_TILE == 0
        nj = GRID_W // COLS_PER_TILE
        x_view = tokens.reshape(bsz, rows, GRID_W, D_MODEL)
        grid = (bsz, nj)
        in_specs = [pl.BlockSpec((1, rows, COLS_PER_TILE, D_MODEL), lambda b, j: (b, 0, j, 0)),
                    pl.BlockSpec((1, 1, D_MODEL), lambda b, j: (b, 0, 0)),
                    pl.BlockSpec((1, 1, D_MODEL), lambda b, j: (b, 0, 0))] + consts

        def rast(width, dtype):
            return (jax.ShapeDtypeStruct((bsz, rows, GRID_W, width), dtype),
                    pl.BlockSpec((1, rows, COLS_PER_TILE, width), lambda b, j: (b, 0, j, 0)))

        def colm(width, dtype):
            return (jax.ShapeDtypeStruct((bsz, GRID_W, rows, width), dtype),
                    pl.BlockSpec((1, COLS_PER_TILE, rows, width), lambda b, j: (b, j, 0, 0)))
    else:
        n_tok = tokens.shape[0]
        assert n_tok % TOK_TILE == 0
        x_view = tokens
        grid = (n_tok // TOK_TILE,)
        in_specs = [pl.BlockSpec((TOK_TILE, D_MODEL), lambda i: (i, 0)),
                    pl.BlockSpec((1, 1, D_MODEL), lambda i: (0, 0, 0)),
                    pl.BlockSpec((1, 1, D_MODEL), lambda i: (0, 0, 0))] + consts

        def rast(width, dtype):
            return (jax.ShapeDtypeStruct((n_tok, width), dtype),
                    pl.BlockSpec((TOK_TILE, width), lambda i: (i, 0)))

        colm = rast

    outs = [rast(3 * QK, BF16), rast(LANES, F32), colm(QK, BF16), colm(2 * QK, F32),
            colm(2 * QK, BF16), colm(QK, BF16)]
    res = pl.pallas_call(
        functools.partial(_pre_kernel, cm=cm),
        grid=grid,
        in_specs=in_specs,
        out_specs=[o[1] for o in outs],
        out_shape=[o[0] for o in outs],
        compiler_params=pltpu.CompilerParams(vmem_limit_bytes=VMEM_LIMIT),
        name="pre_lat" if cm else "pre_ctx",
    )(x_view, shift, scale, w_pre, alog_vec, dtb_vec, lbp)
    return res


def _iota2(n):
    return (lax.broadcasted_iota(jnp.int32, (n, n), 0), lax.broadcasted_iota(jnp.int32, (n, n), 1))


def _scan_masks(rev):
    ii, jj = _iota2(SUPER)
    same = (ii ^ jj) < CHUNK
    incl = same & ((jj >= ii) if rev else (jj <= ii))
    strict = same & ((jj > ii) if rev else (jj < ii))
    return ii, jj, same, incl, strict


def _dot_split2(m01, g):
    m = m01.astype(BF16)
    g1 = g.astype(BF16)
    g2 = (g - g1.astype(F32)).astype(BF16)
    return jnp.dot(m, g1, preferred_element_type=F32) + jnp.dot(m, g2, preferred_element_type=F32)


def _chunk_rows(t, offset, span):
    return jnp.concatenate([jnp.broadcast_to(t[span * i + offset: span * i + offset + 1, :], (span, t.shape[1]))
                            for i in range(SUPER // span)], axis=0)


def _gdn_kernel(ql_ref, kl_ref, vl_ref, qc_ref, kc_ref, vc_ref, abl_ref, abc_ref,
                wq_ref, wk_ref, wv_ref, o_ref,
                q_s, k_s, v_s, wq_s, u_s, kdt_s, at_s, gl_s, st_s, *, n_ctx, n_lat):
    head0 = pl.program_id(1) * HP
    assert n_ctx == SUPER and n_lat % (2 * SUPER) == 0
    ns_lat = n_lat // SUPER
    cps = SUPER // CHUNK
    width = HP * DH

    halo = 2 * SUBLANES

    def conv_block(kind, src_ref, w_ref, dst_ref, src_r0, dst_r0, left_edge, right_edge):
        lo = 0 if left_edge else halo
        hi = 0 if right_edge else halo
        start = src_r0 - lo if isinstance(src_r0, int) else pl.multiple_of(src_r0 - lo, halo)
        parts = [src_ref[0, pl.ds(start, SUPER + lo + hi), :].astype(F32)]
        if left_edge:
            parts.insert(0, jnp.zeros((halo, width), F32))
        if right_edge:
            parts.append(jnp.zeros((halo, width), F32))
        xv = jnp.concatenate(parts, axis=0) if len(parts) > 1 else parts[0]
        acc = jnp.zeros((SUPER, width), F32)
        for j in range(CONV_K):
            sh = (CONV_K // 2 - j) % (SUPER + 2 * halo)
            tap = xv if sh == 0 else pltpu.roll(xv, sh, 0)
            acc = acc + tap[halo:halo + SUPER, :] * w_ref[j:j + 1, :]
        y = _silu(acc)
        for hh in range(HP):
            yh = y[:, hh * DH:(hh + 1) * DH]
            if kind != "v":
                yh = yh * lax.rsqrt(jnp.sum(yh * yh, axis=-1, keepdims=True) + L2_EPS)
            if kind == "q":
                yh = yh * DH ** -0.5
            dst_ref[pl.ds(dst_r0, SUPER), hh * DH:(hh + 1) * DH] = yh.astype(dst_ref.dtype)

    streams = (("q", ql_ref, qc_ref, wq_ref, q_s), ("k", kl_ref, kc_ref, wk_ref, k_s),
               ("v", vl_ref, vc_ref, wv_ref, v_s))

    def conv(blocks):
        for sc, left_edge, right_edge in blocks:
            for kind, lat_ref, ctx_ref, w_ref, dst_ref in streams:
                if isinstance(sc, int) and sc == 0:
                    conv_block(kind, ctx_ref, w_ref, dst_ref, 0, 0, True, True)
                else:
                    conv_block(kind, lat_ref, w_ref, dst_ref, (sc - 1) * SUPER,
                               sc * SUPER if isinstance(sc, int) else pl.multiple_of(sc * SUPER, SUPER),
                               left_edge, right_edge)
                yield

    lane = lax.broadcasted_iota(jnp.int32, (SUPER, LANES), 1)

    def prep(items, slot):
        idx = range(len(items))
        masks = {d: _scan_masks(d == 1) for d in {it[1] for it in items}}
        incl = [masks[it[1]][3] for it in items]
        strict = [masks[it[1]][4] for it in items]
        revs = [it[1] == 1 for it in items]
        r0 = [pl.multiple_of(it[3] * SUPER, SUPER) for it in items]
        hs = [slice(it[0] * DH, (it[0] + 1) * DH) for it in items]
        q = [q_s[pl.ds(r0[i], SUPER), hs[i]].astype(F32) for i in idx]
        k = [k_s[pl.ds(r0[i], SUPER), hs[i]].astype(F32) for i in idx]
        v = [v_s[pl.ds(r0[i], SUPER), hs[i]].astype(F32) for i in idx]
        kq = [_dot_nt(jnp.concatenate([k[i], q[i]], axis=0), k[i]) for i in idx]
        cols = [it[1] * HEADS + head0 + it[0] for it in items]
        pick = lambda ab, col: jnp.broadcast_to(
            jnp.sum(jnp.where(lane == col, ab, 0.0), axis=-1, keepdims=True), (SUPER, DH))
        beta = [pick(items[i][2], 2 * HEADS + cols[i]) for i in idx]
        lower = jnp.where(masks[0][3] if 0 in masks else _scan_masks(False)[3], 1.0, 0.0)
        prefix = {}
        for it in items:
            if id(it[2]) not in prefix:
                prefix[id(it[2])] = _dot_split2(lower, it[2])
        gcum = []
        for i in idx:
            pre = pick(prefix[id(items[i][2])], cols[i])
            if revs[i]:
                pre = _chunk_rows(pre, CHUNK - 1, CHUNK) - pre + pick(items[i][2], cols[i])
            gcum.append(pre)
        yield
        gtot = [_chunk_rows(gcum[i], 0 if revs[i] else CHUNK - 1, CHUNK) for i in idx]
        grow = [jnp.transpose(gcum[i])[0:1, :] for i in idx]
        decay = [jnp.where(incl[i], jnp.exp(jnp.where(
            incl[i], jnp.concatenate([gcum[i], gcum[i]], axis=1) - grow[i], 0.0)), 0.0) for i in idx]
        e_g = [jnp.exp(gcum[i]) for i in idx]
        a_mat = [jnp.where(strict[i], kq[i][:SUPER] * jnp.concatenate([beta[i], beta[i]], axis=1) * decay[i], 0.0)
                 for i in idx]
        attn = [kq[i][SUPER:] * decay[i] for i in idx]
        for c in range(cps):
            rs = slice(c * CHUNK, (c + 1) * CHUNK)
            for i in idx:
                at_s[slot, items[i][0], items[i][1], items[i][4], rs, :] = attn[i][rs, rs].astype(BF16)
        halves = [(i, h) for i in idx for h in range(SUPER // HALF)]
        a_half = [a_mat[i][h * HALF:(h + 1) * HALF, h * HALF:(h + 1) * HALF] for i, h in halves]
        ii, jj = _iota2(HALF)
        eye = jnp.where(ii == jj, 1.0, 0.0).astype(F32)
        t = None
        s = 1
        while s < CHUNK:
            a_off = []
            for n, (i, _) in enumerate(halves):
                row, col = (jj, ii) if revs[i] else (ii, jj)
                m = ((row ^ col) < 2 * s) & ((row & s) != 0) & ((col & s) == 0)
                a_off.append(jnp.where(m, a_half[n], 0.0))
            if t is None:
                t = [eye - a for a in a_off]
            else:
                p = [_dot(a_off[n], t[n]) for n in range(len(halves))]
                yield
                t = [t[n] - _dot(t[n], p[n]) for n in range(len(halves))]
                yield
            s *= 2
        rhs = [jnp.concatenate([k[i] * beta[i] * e_g[i], v[i] * beta[i]], axis=1) for i in idx]
        sol_half = [_dot(t[n], rhs[i][h * HALF:(h + 1) * HALF]) for n, (i, h) in enumerate(halves)]
        sol = [jnp.concatenate([sol_half[n] for n, (j, _) in enumerate(halves) if j == i], axis=0)
               for i in idx]
        qg = [q[i] * e_g[i] for i in idx]
        kdt = [jnp.transpose(k[i] * jnp.exp(gtot[i] - gcum[i])) for i in idx]
        yield
        for i in idx:
            hh, d, e = items[i][0], items[i][1], items[i][4]
            for c in range(cps):
                rs = slice(c * CHUNK, (c + 1) * CHUNK)
                wq_s[slot, hh, d, e, c] = jnp.concatenate([sol[i][rs, :DH], qg[i][rs, :]], axis=0).astype(BF16)
            u_s[slot, hh, d, e] = sol[i][:, DH:]
            kdt_s[slot, hh, d, e] = kdt[i].astype(BF16)
            gl_s[slot, hh, d, e] = jnp.concatenate(
                [jnp.exp(gtot[i][c * CHUNK:c * CHUNK + 1, :]) for c in range(cps)]
                + [jnp.zeros((SUBLANES - cps, DH), F32)], axis=0)

    def scan(chains, slot, mode):
        idx = range(len(chains))
        st = [st_s[ch[0], ch[1]] for ch in chains]
        for e in range(len(chains[0][2])):
            gl = [gl_s[slot, ch[0], ch[1], e] for ch in chains]
            for step_c in range(cps):
                cs = [(cps - 1 - step_c) if ch[1] == 1 else step_c for ch in chains]
                rs = [slice(c * CHUNK, (c + 1) * CHUNK) for c in cs]
                ws = [jnp.dot(wq_s[slot, chains[i][0], chains[i][1], e, cs[i]], st[i].astype(BF16),
                              preferred_element_type=F32) for i in idx]
                yield
                v_new = [(u_s[slot, chains[i][0], chains[i][1], e, rs[i], :] - ws[i][:CHUNK]).astype(BF16)
                         for i in idx]
                st = [st[i] * gl[i][cs[i]:cs[i] + 1, :]
                      + jnp.dot(kdt_s[slot, chains[i][0], chains[i][1], e, :, rs[i]], v_new[i],
                                preferred_element_type=F32) for i in idx]
                if mode is not None:
                    for i in idx:
                        hh, d, scs = chains[i]
                        o = ws[i][CHUNK:] + jnp.dot(at_s[slot, hh, d, e, rs[i], :], v_new[i],
                                                    preferred_element_type=F32)
                        ro = pl.multiple_of(scs[e] * SUPER - n_ctx + cs[i] * CHUNK, CHUNK)
                        if mode == "set":
                            o_ref[0, pl.ds(ro, CHUNK), hh * DH:(hh + 1) * DH] = o
                        else:
                            o_ref[0, pl.ds(ro, CHUNK), hh * DH:(hh + 1) * DH] += o
                yield
        for i in idx:
            st_s[chains[i][0], chains[i][1]] = st[i]

    def interleave(*gens):
        live = list(gens)
        while live:
            for gen in list(live):
                try:
                    next(gen)
                except StopIteration:
                    live.remove(gen)

    def lat_ab(sc):
        return abl_ref[0, pl.ds(pl.multiple_of((sc - 1) * SUPER, SUPER), SUPER), :]

    def group(u, d):
        if isinstance(u, int) and u == 0:
            return [0]
        first = GB * (u - 1)
        return [first + 1 + e if d == 0 else ns_lat - first - e for e in range(GB)]

    def step(u, mode, with_prep, with_conv):
        slot = u % 2
        gens = [scan([(hh, d, group(u, d)) for hh in range(HP) for d in range(2)], slot, mode)]
        if with_prep:
            nxt = [[(sc, lat_ab(sc)) for sc in group(u + 1, d)] for d in range(2)]
            gens.append(prep([(hh, d, nxt[d][e][1], nxt[d][e][0], e)
                              for hh in range(HP) for d in range(2) for e in range(GB)], 1 - slot))
        if with_conv:
            gens.append(conv([(sc, False, False) for d in range(2) for sc in group(u + 2, d)]))
        interleave(*gens)

    assert ns_lat % (2 * GB) == 0 and ns_lat >= 4 * GB
    n_groups = ns_lat // GB
    conv_steps = (ns_lat - 4 * GB) // (2 * GB) + 1
    st_s[...] = jnp.zeros_like(st_s)
    interleave(conv([(0, True, True)]))
    ab_c = abc_ref[0]
    edge = lambda sc: (sc, sc == 1, sc == ns_lat)
    interleave(prep([(hh, d, ab_c, 0, 0) for hh in range(HP) for d in range(2)], 0),
               conv([edge(sc) for d in range(2) for sc in group(1, d)]))
    step(0, None, True, True)

    def run(lo, hi, mode, with_conv):
        def body(u, _):
            step(u, mode, True, with_conv)
            return 0
        lax.fori_loop(lo, hi, body, 0)

    run(1, conv_steps, "set", True)
    run(conv_steps, n_groups // 2 + 1, "set", False)
    run(n_groups // 2 + 1, n_groups, "add", False)
    step(n_groups, "add", False, False)


def _gdn(aqkv_lat, aqkv_ctx, ab_lat, ab_ctx, conv_w):
    bsz, n_lat, _ = aqkv_lat.shape
    n_ctx = aqkv_ctx.shape[1]
    n_all = n_ctx + n_lat
    width = HP * DH
    groups = HEADS // HP
    cps = SUPER // CHUNK

    def stream(n, j0):
        return pl.BlockSpec((1, n, width), lambda b, p: (b, 0, j0 + p))

    def wspec(j0):
        return pl.BlockSpec((CONV_K, width), lambda b, p: (0, j0 + p))

    return pl.pallas_call(
        functools.partial(_gdn_kernel, n_ctx=n_ctx, n_lat=n_lat),
        grid=(bsz, groups),
        in_specs=[stream(n_lat, 0), stream(n_lat, groups), stream(n_lat, 2 * groups),
                  stream(n_ctx, 0), stream(n_ctx, groups), stream(n_ctx, 2 * groups),
                  pl.BlockSpec((1, n_lat, LANES), lambda b, p: (b, 0, 0)),
                  pl.BlockSpec((1, n_ctx, LANES), lambda b, p: (b, 0, 0)),
                  wspec(0), wspec(groups), wspec(2 * groups)],
        out_specs=pl.BlockSpec((1, n_lat, width), lambda b, p: (b, 0, p)),
        out_shape=jax.ShapeDtypeStruct((bsz, n_lat, QK), F32),
        scratch_shapes=[pltpu.VMEM((n_all, width), BF16), pltpu.VMEM((n_all, width), BF16),
                        pltpu.VMEM((n_all, width), BF16),
                        pltpu.VMEM((2, HP, 2, GB, cps, 2 * CHUNK, DH), BF16),
                        pltpu.VMEM((2, HP, 2, GB, SUPER, DH), F32),
                        pltpu.VMEM((2, HP, 2, GB, DH, SUPER), BF16),
                        pltpu.VMEM((2, HP, 2, GB, SUPER, CHUNK), BF16),
                        pltpu.VMEM((2, HP, 2, GB, SUBLANES, DH), F32),
                        pltpu.VMEM((HP, 2, DH, DH), F32)],
        compiler_params=pltpu.CompilerParams(vmem_limit_bytes=VMEM_LIMIT),
        name="gdn",
    )(aqkv_lat, aqkv_lat, aqkv_lat, aqkv_ctx, aqkv_ctx, aqkv_ctx, ab_lat, ab_ctx,
      conv_w, conv_w, conv_w)


def _gla_kernel(ql_ref, gfl_ref, gbl_ref, kfl_ref, kbl_ref, vl_ref,
                qc_ref, gfc_ref, gbc_ref, kfc_ref, kbc_ref, vc_ref,
                o_ref, st_s, *, n_ctx, n_lat):
    assert n_ctx == SUPER and n_lat % (2 * SUPER) == 0
    ns_lat = n_lat // SUPER
    cps = SUPER // CHUNK

    def scan_supers(items, mode):
        idx = range(len(items))
        revs = [it[2] == 1 for it in items]
        masks = {d: _scan_masks(d == 1) for d in {it[2] for it in items}}
        r0 = [pl.multiple_of(it[3] * SUPER, SUPER) for it in items]
        hs = [slice(it[1] * DH, (it[1] + 1) * DH) for it in items]
        g = [items[i][0][1][0, pl.ds(r0[i], SUPER), hs[i]] for i in idx]
        k = [items[i][0][2][0, pl.ds(r0[i], SUPER), hs[i]].astype(F32) for i in idx]
        v = [items[i][0][3][0, pl.ds(r0[i], SUPER), hs[i]] for i in idx]
        lower = jnp.where(_scan_masks(False)[3], 1.0, 0.0)
        pre = {hh: _dot_split2(lower, jnp.concatenate([g[i] for i in idx if items[i][1] == hh], axis=1))
               for hh in sorted({it[1] for it in items})}
        yield
        gcum = [None] * len(items)
        for hh, pr in pre.items():
            for n, i in enumerate([i for i in idx if items[i][1] == hh]):
                p = pr[:, n * DH:(n + 1) * DH]
                gcum[i] = (_chunk_rows(p, CHUNK - 1, CHUNK) - p + g[i]) if revs[i] else p
        g_mid = [_chunk_rows(gcum[i], SUB // 2 if revs[i] else SUB // 2 - 1, SUB) for i in idx]
        g_tot = [_chunk_rows(gcum[i], 0 if revs[i] else CHUNK - 1, CHUNK) for i in idx]
        kd = [k[i] * jnp.exp(g_mid[i] - gcum[i]) for i in idx]
        k_out = [kd[i] * jnp.exp(g_tot[i] - g_mid[i]) for i in idx]
        if mode is not None:
            q = [items[i][0][0][0, pl.ds(r0[i], SUPER), hs[i]].astype(F32) for i in idx]
            row = lax.broadcasted_iota(jnp.int32, (SUPER, DH), 0) & (CHUNK - 1)
            first = [(row >= SUB) if revs[i] else (row < SUB) for i in idx]
            g_bnd = [_chunk_rows(gcum[i], SUB if revs[i] else SUB - 1, CHUNK) for i in idx]
            qd = [q[i] * jnp.exp(gcum[i] - g_mid[i]) for i in idx]
            e_b = [jnp.exp(jnp.where(first[i], g_bnd[i] - g_mid[i], g_mid[i] - g_bnd[i])) for i in idx]
            qo = [jnp.where(first[i], 0.0, qd[i] * e_b[i]) for i in idx]
            ko = [jnp.where(first[i], kd[i] * e_b[i], 0.0) for i in idx]
            sc1 = [_dot_nt(qd[i], kd[i]) for i in idx]
            sc2 = [_dot_nt(qo[i], ko[i]) for i in idx]
            q_in = [qd[i] * jnp.exp(g_mid[i]) for i in idx]
            yield
            attn = []
            for i in idx:
                ii, jj, same, incl, _ = masks[items[i][2]]
                attn.append(jnp.where(incl & ((ii ^ jj) < SUB), sc1[i], 0.0) + jnp.where(same, sc2[i], 0.0))
            o_intra = [_dot(attn[i], v[i]) for i in idx]
        d_st = [[_dot(jnp.transpose(v[i][c * CHUNK:(c + 1) * CHUNK].astype(F32)), k_out[i][c * CHUNK:(c + 1) * CHUNK])
                 for c in range(cps)] for i in idx]
        yield
        st = [st_s[it[1], it[2]] for it in items]
        st_in = [[None] * cps for _ in idx]
        for step_c in range(cps):
            for i in idx:
                c = (cps - 1 - step_c) if revs[i] else step_c
                st_in[i][c] = st[i]
                st[i] = st[i] * jnp.exp(g_tot[i][c * CHUNK:c * CHUNK + 1, :]) + d_st[i][c]
        outs = None
        if mode is not None:
            outs = [[o_intra[i][c * CHUNK:(c + 1) * CHUNK] + _dot_nt(q_in[i][c * CHUNK:(c + 1) * CHUNK], st_in[i][c])
                     for c in range(cps)] for i in idx]
        for i in idx:
            _, hh, d, _ = items[i]
            st_s[hh, d] = st[i]
            if mode == "set":
                o_ref[0, pl.ds(r0[i], SUPER), hs[i]] = jnp.concatenate(outs[i], axis=0)
            elif mode == "add":
                o_ref[0, pl.ds(r0[i], SUPER), hs[i]] += jnp.concatenate(outs[i], axis=0)

    def interleave(*gens):
        live = list(gens)
        while live:
            for gen in list(live):
                try:
                    next(gen)
                except StopIteration:
                    live.remove(gen)

    lat = ((ql_ref, gfl_ref, kfl_ref, vl_ref), (ql_ref, gbl_ref, kbl_ref, vl_ref))
    ctx = ((qc_ref, gfc_ref, kfc_ref, vc_ref), (qc_ref, gbc_ref, kbc_ref, vc_ref))
    st_s[...] = jnp.zeros_like(st_s)
    interleave(scan_supers([(ctx[d], hh, d, 0) for hh in range(HP) for d in range(2)], None))

    def run(lo, hi, mode):
        def body(n, _):
            interleave(*[scan_supers([(lat[d], hh, d, (2 * n + e) if d == 0 else (ns_lat - 1 - 2 * n - e))
                                      for hh in range(HP) for d in range(2)], mode) for e in range(2)])
            return 0
        lax.fori_loop(lo, hi, body, 0)

    assert ns_lat % 4 == 0
    run(0, ns_lat // 4, "set")
    run(ns_lat // 4, ns_lat // 2, "add")


def _gla(lat, ctx):
    bsz, n_lat, _ = lat[0].shape
    n_ctx = ctx[0].shape[1]
    width = HP * DH
    groups = HEADS // HP

    def specs(n):
        one = lambda j0: pl.BlockSpec((1, n, width), lambda b, p: (b, 0, j0 + p))
        return [one(0), one(0), one(groups), one(0), one(groups), one(0)]

    def args(t):
        q, g, k, v = t
        return [q, g, g, k, k, v]

    return pl.pallas_call(
        functools.partial(_gla_kernel, n_ctx=n_ctx, n_lat=n_lat),
        grid=(bsz, groups),
        in_specs=specs(n_lat) + specs(n_ctx),
        out_specs=pl.BlockSpec((1, n_lat, width), lambda b, p: (b, 0, p)),
        out_shape=jax.ShapeDtypeStruct((bsz, n_lat, QK), F32),
        scratch_shapes=[pltpu.VMEM((HP, 2, DH, DH), F32)],
        compiler_params=pltpu.CompilerParams(vmem_limit_bytes=VMEM_LIMIT),
        name="gla",
    )(*args(lat), *args(ctx))


def _gated_rms(o, gain, gate):
    parts = []
    for hh in range(HEADS):
        oh = o[:, hh * DH:(hh + 1) * DH]
        ms = jnp.mean(oh * oh, axis=-1, keepdims=True)
        parts.append(oh * lax.rsqrt(ms + RMS_EPS))
    return jnp.concatenate(parts, axis=1) * gain * _silu(gate)


def _post_kernel(x_ref, oa_ref, ob_ref, shift_ref, scale_ref, gate_ref, w_ref, wa_ref, wb_ref, wo_ref,
                 ga_ref, gb_ref, lng_ref, lnb_ref, out_ref, ob_s):
    for cl in range(COLS_PER_TILE):
        ob_s[:, cl, :] = ob_ref[0, cl]

    def part_stages(part):
        rows = GRID_W // POST_PARTS
        xt = _load_raster(x_ref, True, part, POST_PARTS)
        u = (_layer_norm(xt) * (1.0 + scale_ref[0]) + shift_ref[0]).astype(BF16)
        yield
        za = jnp.dot(u, w_ref[:, 0:QK], preferred_element_type=F32)
        y_a = _dot(_gated_rms(_load_raster(oa_ref, True, part, POST_PARTS), ga_ref[...], za), wa_ref[...])
        yield
        zb = jnp.dot(u, w_ref[:, QK:2 * QK], preferred_element_type=F32)
        ob = ob_s[part * rows:(part + 1) * rows].reshape(TOK_TILE // POST_PARTS, QK)
        y_b = _dot(_gated_rms(ob, gb_ref[...], zb), wb_ref[...])
        yield
        mix = _sigmoid(jnp.dot(u, w_ref[:, 2 * QK:2 * QK + D_MODEL], preferred_element_type=F32)) * y_a
        yield
        mix = mix + _sigmoid(jnp.dot(u, w_ref[:, 2 * QK + D_MODEL:], preferred_element_type=F32)) * y_b
        yield
        sub = _dot(mix, wo_ref[...])
        hres = DEEPNORM_ALPHA * xt + gate_ref[0] * sub
        y = _layer_norm(hres) * lng_ref[...] + lnb_ref[...]
        _store_raster(out_ref, y, True, part, POST_PARTS)

    _interleave_skewed([part_stages(p) for p in range(POST_PARTS)])


def _post_project(x, oa, ob_cm, shift, scale, gate, w_post, w_a_out, w_b_out, w_out, a_gain, b_gain, ln_g, ln_b):
    bsz, length, _ = x.shape
    rows = length // GRID_W
    nj = GRID_W // COLS_PER_TILE
    const = lambda shape: pl.BlockSpec(shape, lambda b, j: tuple(0 for _ in shape))
    rast = lambda width: pl.BlockSpec((1, rows, COLS_PER_TILE, width), lambda b, j: (b, 0, j, 0))
    modv = pl.BlockSpec((1, 1, D_MODEL), lambda b, j: (b, 0, 0))
    out = pl.pallas_call(
        _post_kernel,
        grid=(bsz, nj),
        in_specs=[rast(D_MODEL), rast(QK),
                  pl.BlockSpec((1, COLS_PER_TILE, rows, QK), lambda b, j: (b, j, 0, 0)),
                  modv, modv, modv,
                  const((D_MODEL, N_POST)), const((QK, D_MODEL)), const((QK, D_MODEL)),
                  const((D_MODEL, D_MODEL)), const((1, QK)), const((1, QK)),
                  const((1, D_MODEL)), const((1, D_MODEL))],
        out_specs=rast(D_MODEL),
        out_shape=jax.ShapeDtypeStruct((bsz, rows, GRID_W, D_MODEL), F32),
        scratch_shapes=[pltpu.VMEM((rows, COLS_PER_TILE, QK), F32)],
        compiler_params=pltpu.CompilerParams(vmem_limit_bytes=VMEM_LIMIT),
        name="post",
    )(x.reshape(bsz, rows, GRID_W, D_MODEL), oa.reshape(bsz, rows, GRID_W, QK),
      ob_cm.reshape(bsz, GRID_W, rows, QK), shift, scale, gate,
      w_post, w_a_out, w_b_out, w_out, a_gain, b_gain, ln_g, ln_b)
    return out.reshape(bsz, length, D_MODEL)


def kernel(x, c, ctx, c_ctx, w_mod, b_mod, w_in, conv_w, a_log, dt_bias, lb_param, a_norm_g, b_norm_g,
           w_a_out, w_b_out, w_out, ln_g, ln_b):
    assert w_mod.shape[0] == DEPTH
    bsz, length, _ = x.shape
    n_ctx = ctx.shape[1]
    f32 = lambda t: t.astype(F32)

    w = f32(w_in[0])
    o_alpha = 3 * QK
    o_agate = o_alpha + 4 * HEADS
    o_bq = o_agate + QK
    o_bgate = o_bq + 4 * QK
    w_pre = jnp.concatenate([w[:, :o_alpha], jnp.pad(w[:, o_alpha:o_agate], ((0, 0), (0, LANES - 4 * HEADS))),
                             w[:, o_bq:o_bgate]], axis=1).astype(BF16)
    w_post = jnp.concatenate([w[:, o_agate:o_bq], w[:, o_bgate:]], axis=1).astype(BF16)
    alog_vec = jnp.pad(f32(a_log[0]).reshape(1, 2 * HEADS), ((0, 0), (0, LANES - 2 * HEADS)))
    dtb_vec = jnp.pad(f32(dt_bias[0]).reshape(1, 2 * HEADS), ((0, 0), (0, LANES - 2 * HEADS)))
    lbp = f32(lb_param).reshape(DEPTH + 1, 2 * QK)

    mod = _modulation(f32(c), f32(c_ctx), f32(w_mod[0]), f32(b_mod[0]))
    shift_l = mod[:bsz, None, 0:D_MODEL]
    scale_l = mod[:bsz, None, D_MODEL:2 * D_MODEL]
    gate_l = mod[:bsz, None, 2 * D_MODEL:]
    shift_c = mod[bsz:bsz + 1, None, 0:D_MODEL]
    scale_c = mod[bsz:bsz + 1, None, D_MODEL:2 * D_MODEL]

    lat = _pre_project(f32(x), shift_l, scale_l, w_pre, alog_vec, dtb_vec, lbp, cm=True)
    cx = _pre_project(f32(ctx).reshape(bsz * n_ctx, D_MODEL), shift_c, scale_c, w_pre, alog_vec, dtb_vec, lbp,
                      cm=False)
    aqkv_l = lat[0].reshape(bsz, length, 3 * QK)
    ab_l = lat[1].reshape(bsz, length, LANES)
    gla_l = tuple(t.reshape(bsz, length, t.shape[-1]) for t in lat[2:])
    aqkv_c = cx[0].reshape(bsz, n_ctx, 3 * QK)
    ab_c = cx[1].reshape(bsz, n_ctx, LANES)
    gla_c = tuple(t.reshape(bsz, n_ctx, t.shape[-1]) for t in cx[2:])

    oa = _gdn(aqkv_l, aqkv_c, ab_l, ab_c, f32(conv_w[0]))
    ob_cm = _gla(gla_l, gla_c)

    a_gain = jnp.tile(f32(a_norm_g[0]), HEADS).reshape(1, QK)
    b_gain = jnp.tile(f32(b_norm_g[0]), HEADS).reshape(1, QK)
    out = _post_project(f32(x), oa, ob_cm, shift_l, scale_l, gate_l, w_post,
                        f32(w_a_out[0]).astype(BF16), f32(w_b_out[0]).astype(BF16), f32(w_out[0]).astype(BF16),
                        a_gain, b_gain, f32(ln_g[0]).reshape(1, D_MODEL), f32(ln_b[0]).reshape(1, D_MODEL))
    return out.astype(x.dtype)
```

```python
import functools

import jax
import jax.numpy as jnp
from jax import lax
from jax.experimental import pallas as pl
from jax.experimental.pallas import tpu as pltpu

D_MODEL = 1024
GRID_W = 64
HEADS = 4
DH = 128
QK = HEADS * DH
CONV_K = 5
CHUNK = 64
SUPER = 256
HALF = 128
SUB = 32
HP = 2
GB = 2
GLA_GROUP = 4
DEPTH = 1
DEEPNORM_ALPHA = (2 * DEPTH) ** 0.25
LN_EPS = 1e-6
RMS_EPS = 1e-6
L2_EPS = 1e-6

LANES = 128
SUBLANES = 8
VMEM_BYTES = 64 * 1024 * 1024

TOK_TILE = 1024
COLS_PER_TILE = TOK_TILE // GRID_W
PRE_PARTS = 8
POST_PARTS = 2
N_PRE = 3 * QK + LANES + 4 * QK
N_POST = 2 * QK + 2 * D_MODEL
VMEM_LIMIT = VMEM_BYTES - 6 * 1024 * 1024

F32 = jnp.float32
BF16 = jnp.bfloat16


def _dot(a, b):
    return jnp.dot(a.astype(BF16), b.astype(BF16), preferred_element_type=F32)


def _dot_nt(a, b):
    return lax.dot_general(a.astype(BF16), b.astype(BF16), (((1,), (1,)), ((), ())),
                           preferred_element_type=F32)


def _dot_exact(a, b):
    return jnp.dot(a, b, preferred_element_type=F32, precision=lax.Precision.HIGHEST)


def _sigmoid(x):
    return 1.0 / (1.0 + jnp.exp(-x))


def _sigmoid_pair(x):
    t = jnp.exp(-jnp.abs(x))
    r = 1.0 / (1.0 + t)
    tr = t * r
    pos = x >= 0
    return jnp.where(pos, r, tr), jnp.where(pos, tr, r)


def _silu(x):
    return x * _sigmoid(x)


def _layer_norm(t):
    mu = jnp.mean(t, axis=-1, keepdims=True)
    tc = t - mu
    var = jnp.mean(tc * tc, axis=-1, keepdims=True)
    return tc * lax.rsqrt(var + LN_EPS)


def _mod_kernel(c_ref, w_ref, b_ref, o_ref):
    o_ref[...] = _dot_exact(_silu(c_ref[...]), w_ref[...]) + b_ref[...]


def _modulation(c, c_ctx, w_mod, b_mod):
    bsz = c.shape[0]
    rows = -(-(bsz + 1) // SUBLANES) * SUBLANES
    cc = jnp.zeros((rows, D_MODEL), F32).at[:bsz].set(c).at[bsz].set(c_ctx)
    nblk = 3
    out = pl.pallas_call(
        _mod_kernel,
        grid=(nblk,),
        in_specs=[pl.BlockSpec((rows, D_MODEL), lambda j: (0, 0)),
                  pl.BlockSpec((D_MODEL, D_MODEL), lambda j: (0, j)),
                  pl.BlockSpec((1, D_MODEL), lambda j: (0, j))],
        out_specs=pl.BlockSpec((rows, D_MODEL), lambda j: (0, j)),
        out_shape=jax.ShapeDtypeStruct((rows, 3 * D_MODEL), F32),
        name="mod",
    )(cc, w_mod, b_mod.reshape(1, 3 * D_MODEL))
    return out


def _interleave_skewed(gens):
    live = list(enumerate(gens))
    rnd = 0
    while live:
        for i, gen in list(live):
            if rnd >= i:
                try:
                    next(gen)
                except StopIteration:
                    live.remove((i, gen))
        rnd += 1


def _load_raster(ref, grid_tile, part, parts):
    tok = TOK_TILE // parts
    if grid_tile:
        rows = GRID_W // parts
        return ref[0, part * rows:(part + 1) * rows].reshape(tok, ref.shape[-1])
    return ref[part * tok:(part + 1) * tok, :]


def _load_colmajor(ref, part, parts):
    cols = COLS_PER_TILE // parts
    return jnp.concatenate([ref[0, :, cl, :] for cl in range(part * cols, (part + 1) * cols)], axis=0)


def _store_raster(ref, val, grid_tile, part, parts, lanes=slice(None)):
    tok = TOK_TILE // parts
    if grid_tile:
        rows = GRID_W // parts
        ref[0, part * rows:(part + 1) * rows, :, lanes] = (
            val.reshape(rows, COLS_PER_TILE, val.shape[-1]).astype(ref.dtype))
    else:
        ref[part * tok:(part + 1) * tok, lanes] = val.astype(ref.dtype)


def _store_colmajor(ref, val, grid_tile, part, parts, lanes=slice(None)):
    tok = TOK_TILE // parts
    if grid_tile:
        cols = COLS_PER_TILE // parts
        for n, cl in enumerate(range(part * cols, (part + 1) * cols)):
            ref[0, cl, :, lanes] = val[n * GRID_W:(n + 1) * GRID_W, :].astype(ref.dtype)
    else:
        ref[part * tok:(part + 1) * tok, lanes] = val.astype(ref.dtype)


def _pre_kernel(x_ref, shift_ref, scale_ref, w_ref, alog_ref, dtb_ref, lbp_ref,
                aqkv_ref, ab_ref, bq_ref, bg_ref, bk_ref, bi_ref, *, cm):
    modulate = lambda t: (_layer_norm(t) * (1.0 + scale_ref[0]) + shift_ref[0]).astype(BF16)

    def part_stages(part):
        u = modulate(_load_raster(x_ref, cm, part, PRE_PARTS))
        yield
        for g in range(3):
            z = jnp.dot(u, w_ref[:, g * QK:(g + 1) * QK], preferred_element_type=F32)
            _store_raster(aqkv_ref, z, cm, part, PRE_PARTS, slice(g * QK, (g + 1) * QK))
            yield
        off = 3 * QK
        z = jnp.dot(u, w_ref[:, off:off + LANES], preferred_element_type=F32)
        zs = z + dtb_ref[...]
        softplus = jnp.maximum(zs, 0.0) + jnp.log(1.0 + jnp.exp(-jnp.abs(zs)))
        a_g = -jnp.exp(alog_ref[...]) * softplus
        a_b = _sigmoid(z)
        lane = lax.broadcasted_iota(jnp.int32, z.shape, 1)
        _store_raster(ab_ref, jnp.where(lane < 2 * HEADS, a_g, a_b), cm, part, PRE_PARTS)
        u_b = modulate(_load_colmajor(x_ref, part, PRE_PARTS)) if cm else u
        yield
        off += LANES
        z = jnp.dot(u_b, w_ref[:, off:off + QK], preferred_element_type=F32)
        _store_colmajor(bq_ref, _silu(z) * DH ** -0.5, cm, part, PRE_PARTS)
        yield
        off += QK
        p0 = lbp_ref[0:1, :]
        p1 = lbp_ref[1:2, :]
        pm = jnp.maximum(p0, p1)
        e0 = jnp.exp(p0 - pm)
        lb = e0 / (e0 + jnp.exp(p1 - pm))
        for d in range(2):
            z = jnp.dot(u_b, w_ref[:, off + d * QK: off + (d + 1) * QK], preferred_element_type=F32)
            s_pos, s_neg = _sigmoid_pair(z)
            lbd = lb[:, d * QK:(d + 1) * QK]
            lanes = slice(d * QK, (d + 1) * QK)
            _store_colmajor(bg_ref, jnp.log(lbd + (1.0 - lbd) * s_pos), cm, part, PRE_PARTS, lanes)
            _store_colmajor(bk_ref, (1.0 - lbd) * s_neg, cm, part, PRE_PARTS, lanes)
            yield
        off += 2 * QK
        z = jnp.dot(u_b, w_ref[:, off:off + QK], preferred_element_type=F32)
        _store_colmajor(bi_ref, z, cm, part, PRE_PARTS)

    _interleave_skewed([part_stages(p) for p in range(PRE_PARTS)])


def _pre_project(tokens, shift, scale, w_pre, alog_vec, dtb_vec, lbp, *, cm):
    consts = [pl.BlockSpec((D_MODEL, N_PRE), lambda *_: (0, 0)),
              pl.BlockSpec((1, LANES), lambda *_: (0, 0)),
              pl.BlockSpec((1, LANES), lambda *_: (0, 0)),
              pl.BlockSpec((2, 2 * QK), lambda *_: (0, 0))]
    if cm:
        bsz, length, _ = tokens.shape
        rows = length // GRID_W
        assert rows == GRID_W and GRID_W % COLS_PER_TILE == 0
        nj = GRID_W // COLS_PER_TILE
        x_view = tokens.reshape(bsz, rows, GRID_W, D_MODEL)
        grid = (bsz, nj)
        in_specs = [pl.BlockSpec((1, rows, COLS_PER_TILE, D_MODEL), lambda b, j: (b, 0, j, 0)),
                    pl.BlockSpec((1, 1, D_MODEL), lambda b, j: (b, 0, 0)),
                    pl.BlockSpec((1, 1, D_MODEL), lambda b, j: (b, 0, 0))] + consts

        def rast(width, dtype):
            return (jax.ShapeDtypeStruct((bsz, rows, GRID_W, width), dtype),
                    pl.BlockSpec((1, rows, COLS_PER_TILE, width), lambda b, j: (b, 0, j, 0)))

        def colm(width, dtype):
            return (jax.ShapeDtypeStruct((bsz, GRID_W, rows, width), dtype),
                    pl.BlockSpec((1, COLS_PER_TILE, rows, width), lambda b, j: (b, j, 0, 0)))
    else:
        n_tok = tokens.shape[0]
        assert n_tok % TOK_TILE == 0
        x_view = tokens
        grid = (n_tok // TOK_TILE,)
        in_specs = [pl.BlockSpec((TOK_TILE, D_MODEL), lambda i: (i, 0)),
                    pl.BlockSpec((1, 1, D_MODEL), lambda i: (0, 0, 0)),
                    pl.BlockSpec((1, 1, D_MODEL), lambda i: (0, 0, 0))] + consts

        def rast(width, dtype):
            return (jax.ShapeDtypeStruct((n_tok, width), dtype),
                    pl.BlockSpec((TOK_TILE, width), lambda i: (i, 0)))

        colm = rast

    outs = [rast(3 * QK, BF16), rast(LANES, F32), colm(QK, BF16), colm(2 * QK, F32),
            colm(2 * QK, BF16), colm(QK, BF16)]
    res = pl.pallas_call(
        functools.partial(_pre_kernel, cm=cm),
        grid=grid,
        in_specs=in_specs,
        out_specs=[o[1] for o in outs],
        out_shape=[o[0] for o in outs],
        compiler_params=pltpu.CompilerParams(vmem_limit_bytes=VMEM_LIMIT),
        name="pre_lat" if cm else "pre_ctx",
    )(x_view, shift, scale, w_pre, alog_vec, dtb_vec, lbp)
    return res


def _iota2(n):
    return (lax.broadcasted_iota(jnp.int32, (n, n), 0), lax.broadcasted_iota(jnp.int32, (n, n), 1))


def _scan_masks(rev):
    ii, jj = _iota2(SUPER)
    same = (ii ^ jj) < CHUNK
    incl = same & ((jj >= ii) if rev else (jj <= ii))
    strict = same & ((jj > ii) if rev else (jj < ii))
    return ii, jj, same, incl, strict


def _dot_split2(m01, g):
    m = m01.astype(BF16)
    g1 = g.astype(BF16)
    g2 = (g - g1.astype(F32)).astype(BF16)
    return jnp.dot(m, g1, preferred_element_type=F32) + jnp.dot(m, g2, preferred_element_type=F32)


def _chunk_rows(t, offset, span):
    return jnp.concatenate([jnp.broadcast_to(t[span * i + offset: span * i + offset + 1, :], (span, t.shape[1]))
                            for i in range(SUPER // span)], axis=0)


def _gdn_kernel(ql_ref, kl_ref, vl_ref, qc_ref, kc_ref, vc_ref, abl_ref, abc_ref,
                wq_ref, wk_ref, wv_ref, o_ref,
                q_s, k_s, v_s, wq_s, u_s, kdt_s, at_s, gl_s, st_s, *, n_ctx, n_lat):
    head0 = pl.program_id(1) * HP
    assert n_ctx == SUPER and n_lat % (2 * SUPER) == 0
    ns_lat = n_lat // SUPER
    cps = SUPER // CHUNK
    width = HP * DH

    halo = 2 * SUBLANES

    def conv_block(kind, src_ref, w_ref, dst_ref, src_r0, dst_r0, left_edge, right_edge):
        lo = 0 if left_edge else halo
        hi = 0 if right_edge else halo
        start = src_r0 - lo if isinstance(src_r0, int) else pl.multiple_of(src_r0 - lo, halo)
        parts = [src_ref[0, pl.ds(start, SUPER + lo + hi), :].astype(F32)]
        if left_edge:
            parts.insert(0, jnp.zeros((halo, width), F32))
        if right_edge:
            parts.append(jnp.zeros((halo, width), F32))
        xv = jnp.concatenate(parts, axis=0) if len(parts) > 1 else parts[0]
        acc = jnp.zeros((SUPER, width), F32)
        for j in range(CONV_K):
            sh = (CONV_K // 2 - j) % (SUPER + 2 * halo)
            tap = xv if sh == 0 else pltpu.roll(xv, sh, 0)
            acc = acc + tap[halo:halo + SUPER, :] * w_ref[j:j + 1, :]
        y = _silu(acc)
        for hh in range(HP):
            yh = y[:, hh * DH:(hh + 1) * DH]
            if kind != "v":
                yh = yh * lax.rsqrt(jnp.sum(yh * yh, axis=-1, keepdims=True) + L2_EPS)
            if kind == "q":
                yh = yh * DH ** -0.5
            dst_ref[pl.ds(dst_r0, SUPER), hh * DH:(hh + 1) * DH] = yh.astype(dst_ref.dtype)

    streams = (("q", ql_ref, qc_ref, wq_ref, q_s), ("k", kl_ref, kc_ref, wk_ref, k_s),
               ("v", vl_ref, vc_ref, wv_ref, v_s))

    def conv(blocks):
        for sc, left_edge, right_edge in blocks:
            for kind, lat_ref, ctx_ref, w_ref, dst_ref in streams:
                if isinstance(sc, int) and sc == 0:
                    conv_block(kind, ctx_ref, w_ref, dst_ref, 0, 0, True, True)
                else:
                    conv_block(kind, lat_ref, w_ref, dst_ref, (sc - 1) * SUPER,
                               sc * SUPER if isinstance(sc, int) else pl.multiple_of(sc * SUPER, SUPER),
                               left_edge, right_edge)
                yield

    lane = lax.broadcasted_iota(jnp.int32, (SUPER, LANES), 1)

    def prep(items, slot):
        idx = range(len(items))
        masks = {d: _scan_masks(d == 1) for d in {it[1] for it in items}}
        incl = [masks[it[1]][3] for it in items]
        strict = [masks[it[1]][4] for it in items]
        revs = [it[1] == 1 for it in items]
        r0 = [pl.multiple_of(it[3] * SUPER, SUPER) for it in items]
        hs = [slice(it[0] * DH, (it[0] + 1) * DH) for it in items]
        q = [q_s[pl.ds(r0[i], SUPER), hs[i]].astype(F32) for i in idx]
        k = [k_s[pl.ds(r0[i], SUPER), hs[i]].astype(F32) for i in idx]
        v = [v_s[pl.ds(r0[i], SUPER), hs[i]].astype(F32) for i in idx]
        kq = [_dot_nt(jnp.concatenate([k[i], q[i]], axis=0), k[i]) for i in idx]
        cols = [it[1] * HEADS + head0 + it[0] for it in items]
        pick = lambda ab, col: jnp.broadcast_to(
            jnp.sum(jnp.where(lane == col, ab, 0.0), axis=-1, keepdims=True), (SUPER, DH))
        beta = [pick(items[i][2], 2 * HEADS + cols[i]) for i in idx]
        lower = jnp.where(masks[0][3] if 0 in masks else _scan_masks(False)[3], 1.0, 0.0)
        prefix = {}
        for it in items:
            if id(it[2]) not in prefix:
                prefix[id(it[2])] = _dot_split2(lower, it[2])
        gcum = []
        for i in idx:
            pre = pick(prefix[id(items[i][2])], cols[i])
            if revs[i]:
                pre = _chunk_rows(pre, CHUNK - 1, CHUNK) - pre + pick(items[i][2], cols[i])
            gcum.append(pre)
        yield
        gtot = [_chunk_rows(gcum[i], 0 if revs[i] else CHUNK - 1, CHUNK) for i in idx]
        grow = [jnp.transpose(gcum[i])[0:1, :] for i in idx]
        decay = [jnp.where(incl[i], jnp.exp(jnp.where(
            incl[i], jnp.concatenate([gcum[i], gcum[i]], axis=1) - grow[i], 0.0)), 0.0) for i in idx]
        e_g = [jnp.exp(gcum[i]) for i in idx]
        a_mat = [jnp.where(strict[i], kq[i][:SUPER] * jnp.concatenate([beta[i], beta[i]], axis=1) * decay[i], 0.0)
                 for i in idx]
        attn = [kq[i][SUPER:] * decay[i] for i in idx]
        for c in range(cps):
            rs = slice(c * CHUNK, (c + 1) * CHUNK)
            for i in idx:
                at_s[slot, items[i][0], items[i][1], items[i][4], rs, :] = attn[i][rs, rs].astype(BF16)
        halves = [(i, h) for i in idx for h in range(SUPER // HALF)]
        a_half = [a_mat[i][h * HALF:(h + 1) * HALF, h * HALF:(h + 1) * HALF] for i, h in halves]
        ii, jj = _iota2(HALF)
        eye = jnp.where(ii == jj, 1.0, 0.0).astype(F32)
        t = None
        s = 1
        while s < CHUNK:
            a_off = []
            for n, (i, _) in enumerate(halves):
                row, col = (jj, ii) if revs[i] else (ii, jj)
                m = ((row ^ col) < 2 * s) & ((row & s) != 0) & ((col & s) == 0)
                a_off.append(jnp.where(m, a_half[n], 0.0))
            if t is None:
                t = [eye - a for a in a_off]
            else:
                p = [_dot(a_off[n], t[n]) for n in range(len(halves))]
                yield
                t = [t[n] - _dot(t[n], p[n]) for n in range(len(halves))]
                yield
            s *= 2
        rhs = [jnp.concatenate([k[i] * beta[i] * e_g[i], v[i] * beta[i]], axis=1) for i in idx]
        sol_half = [_dot(t[n], rhs[i][h * HALF:(h + 1) * HALF]) for n, (i, h) in enumerate(halves)]
        sol = [jnp.concatenate([sol_half[n] for n, (j, _) in enumerate(halves) if j == i], axis=0)
               for i in idx]
        qg = [q[i] * e_g[i] for i in idx]
        kdt = [jnp.transpose(k[i] * jnp.exp(gtot[i] - gcum[i])) for i in idx]
        yield
        for i in idx:
            hh, d, e = items[i][0], items[i][1], items[i][4]
            for c in range(cps):
                rs = slice(c * CHUNK, (c + 1) * CHUNK)
                wq_s[slot, hh, d, e, c] = jnp.concatenate([sol[i][rs, :DH], qg[i][rs, :]], axis=0).astype(BF16)
            u_s[slot, hh, d, e] = sol[i][:, DH:]
            kdt_s[slot, hh, d, e] = kdt[i].astype(BF16)
            gl_s[slot, hh, d, e] = jnp.concatenate(
                [jnp.exp(gtot[i][c * CHUNK:c * CHUNK + 1, :]) for c in range(cps)]
                + [jnp.zeros((SUBLANES - cps, DH), F32)], axis=0)

    def scan(chains, slot, mode):
        idx = range(len(chains))
        st = [st_s[ch[0], ch[1]] for ch in chains]
        for e in range(len(chains[0][2])):
            gl = [gl_s[slot, ch[0], ch[1], e] for ch in chains]
            for step_c in range(cps):
                cs = [(cps - 1 - step_c) if ch[1] == 1 else step_c for ch in chains]
                rs = [slice(c * CHUNK, (c + 1) * CHUNK) for c in cs]
                ws = [jnp.dot(wq_s[slot, chains[i][0], chains[i][1], e, cs[i]], st[i].astype(BF16),
                              preferred_element_type=F32) for i in idx]
                yield
                v_new = [(u_s[slot, chains[i][0], chains[i][1], e, rs[i], :] - ws[i][:CHUNK]).astype(BF16)
                         for i in idx]
                st = [st[i] * gl[i][cs[i]:cs[i] + 1, :]
                      + jnp.dot(kdt_s[slot, chains[i][0], chains[i][1], e, :, rs[i]], v_new[i],
                                preferred_element_type=F32) for i in idx]
                if mode is not None:
                    for i in idx:
                        hh, d, scs = chains[i]
                        o = ws[i][CHUNK:] + jnp.dot(at_s[slot, hh, d, e, rs[i], :], v_new[i],
                                                    preferred_element_type=F32)
                        ro = pl.multiple_of(scs[e] * SUPER - n_ctx + cs[i] * CHUNK, CHUNK)
                        if mode == "set":
                            o_ref[0, pl.ds(ro, CHUNK), hh * DH:(hh + 1) * DH] = o
                        else:
                            o_ref[0, pl.ds(ro, CHUNK), hh * DH:(hh + 1) * DH] += o
                yield
        for i in idx:
            st_s[chains[i][0], chains[i][1]] = st[i]

    def interleave(*gens):
        live = list(gens)
        while live:
            for gen in list(live):
                try:
                    next(gen)
                except StopIteration:
                    live.remove(gen)

    def lat_ab(sc):
        return abl_ref[0, pl.ds(pl.multiple_of((sc - 1) * SUPER, SUPER), SUPER), :]

    def group(u, d):
        if isinstance(u, int) and u == 0:
            return [0]
        first = GB * (u - 1)
        return [first + 1 + e if d == 0 else ns_lat - first - e for e in range(GB)]

    def step(u, mode, with_prep, with_conv):
        slot = u % 2
        gens = [scan([(hh, d, group(u, d)) for hh in range(HP) for d in range(2)], slot, mode)]
        if with_prep:
            nxt = [[(sc, lat_ab(sc)) for sc in group(u + 1, d)] for d in range(2)]
            gens.append(prep([(hh, d, nxt[d][e][1], nxt[d][e][0], e)
                              for hh in range(HP) for d in range(2) for e in range(GB)], 1 - slot))
        if with_conv:
            gens.append(conv([(sc, False, False) for d in range(2) for sc in group(u + 2, d)]))
        interleave(*gens)

    assert ns_lat % (2 * GB) == 0 and ns_lat >= 4 * GB
    n_groups = ns_lat // GB
    conv_steps = (ns_lat - 4 * GB) // (2 * GB) + 1
    st_s[...] = jnp.zeros_like(st_s)
    interleave(conv([(0, True, True)]))
    ab_c = abc_ref[0]
    edge = lambda sc: (sc, sc == 1, sc == ns_lat)
    interleave(prep([(hh, d, ab_c, 0, 0) for hh in range(HP) for d in range(2)], 0),
               conv([edge(sc) for d in range(2) for sc in group(1, d)]))
    step(0, None, True, True)

    def run(lo, hi, mode, with_conv):
        def body(u, _):
            step(u, mode, True, with_conv)
            return 0
        lax.fori_loop(lo, hi, body, 0)

    run(1, conv_steps, "set", True)
    run(conv_steps, n_groups // 2 + 1, "set", False)
    run(n_groups // 2 + 1, n_groups, "add", False)
    step(n_groups, "add", False, False)


def _gdn(aqkv_lat, aqkv_ctx, ab_lat, ab_ctx, conv_w):
    bsz, n_lat, _ = aqkv_lat.shape
    n_ctx = aqkv_ctx.shape[1]
    n_all = n_ctx + n_lat
    width = HP * DH
    groups = HEADS // HP
    cps = SUPER // CHUNK

    def stream(n, j0):
        return pl.BlockSpec((1, n, width), lambda b, p: (b, 0, j0 + p))

    def wspec(j0):
        return pl.BlockSpec((CONV_K, width), lambda b, p: (0, j0 + p))

    return pl.pallas_call(
        functools.partial(_gdn_kernel, n_ctx=n_ctx, n_lat=n_lat),
        grid=(bsz, groups),
        in_specs=[stream(n_lat, 0), stream(n_lat, groups), stream(n_lat, 2 * groups),
                  stream(n_ctx, 0), stream(n_ctx, groups), stream(n_ctx, 2 * groups),
                  pl.BlockSpec((1, n_lat, LANES), lambda b, p: (b, 0, 0)),
                  pl.BlockSpec((1, n_ctx, LANES), lambda b, p: (b, 0, 0)),
                  wspec(0), wspec(groups), wspec(2 * groups)],
        out_specs=pl.BlockSpec((1, n_lat, width), lambda b, p: (b, 0, p)),
        out_shape=jax.ShapeDtypeStruct((bsz, n_lat, QK), F32),
        scratch_shapes=[pltpu.VMEM((n_all, width), BF16), pltpu.VMEM((n_all, width), BF16),
                        pltpu.VMEM((n_all, width), BF16),
                        pltpu.VMEM((2, HP, 2, GB, cps, 2 * CHUNK, DH), BF16),
                        pltpu.VMEM((2, HP, 2, GB, SUPER, DH), F32),
                        pltpu.VMEM((2, HP, 2, GB, DH, SUPER), BF16),
                        pltpu.VMEM((2, HP, 2, GB, SUPER, CHUNK), BF16),
                        pltpu.VMEM((2, HP, 2, GB, SUBLANES, DH), F32),
                        pltpu.VMEM((HP, 2, DH, DH), F32)],
        compiler_params=pltpu.CompilerParams(vmem_limit_bytes=VMEM_LIMIT),
        name="gdn",
    )(aqkv_lat, aqkv_lat, aqkv_lat, aqkv_ctx, aqkv_ctx, aqkv_ctx, ab_lat, ab_ctx,
      conv_w, conv_w, conv_w)


def _gla_kernel(ql_ref, gfl_ref, gbl_ref, kfl_ref, kbl_ref, vl_ref,
                qc_ref, gfc_ref, gbc_ref, kfc_ref, kbc_ref, vc_ref,
                o_ref, st_s, *, n_ctx, n_lat):
    assert n_ctx == SUPER and n_lat % (2 * SUPER) == 0
    ns_lat = n_lat // SUPER
    cps = SUPER // CHUNK

    def scan_supers(items, mode):
        idx = range(len(items))
        revs = [it[2] == 1 for it in items]
        masks = {d: _scan_masks(d == 1) for d in {it[2] for it in items}}
        r0 = [pl.multiple_of(it[3] * SUPER, SUPER) for it in items]
        hs = [slice(it[1] * DH, (it[1] + 1) * DH) for it in items]
        g = [items[i][0][1][0, pl.ds(r0[i], SUPER), hs[i]] for i in idx]
        k = [items[i][0][2][0, pl.ds(r0[i], SUPER), hs[i]].astype(F32) for i in idx]
        v = [items[i][0][3][0, pl.ds(r0[i], SUPER), hs[i]] for i in idx]
        lower = jnp.where(_scan_masks(False)[3], 1.0, 0.0)
        pre = {hh: _dot_split2(lower, jnp.concatenate([g[i] for i in idx if items[i][1] == hh], axis=1))
               for hh in sorted({it[1] for it in items})}
        yield
        gcum = [None] * len(items)
        for hh, pr in pre.items():
            for n, i in enumerate([i for i in idx if items[i][1] == hh]):
                p = pr[:, n * DH:(n + 1) * DH]
                gcum[i] = (_chunk_rows(p, CHUNK - 1, CHUNK) - p + g[i]) if revs[i] else p
        g_mid = [_chunk_rows(gcum[i], SUB // 2 if revs[i] else SUB // 2 - 1, SUB) for i in idx]
        g_tot = [_chunk_rows(gcum[i], 0 if revs[i] else CHUNK - 1, CHUNK) for i in idx]
        kd = [k[i] * jnp.exp(g_mid[i] - gcum[i]) for i in idx]
        k_out = [kd[i] * jnp.exp(g_tot[i] - g_mid[i]) for i in idx]
        if mode is not None:
            q = [items[i][0][0][0, pl.ds(r0[i], SUPER), hs[i]].astype(F32) for i in idx]
            row = lax.broadcasted_iota(jnp.int32, (SUPER, DH), 0) & (CHUNK - 1)
            first = [(row >= SUB) if revs[i] else (row < SUB) for i in idx]
            g_bnd = [_chunk_rows(gcum[i], SUB if revs[i] else SUB - 1, CHUNK) for i in idx]
            qd = [q[i] * jnp.exp(gcum[i] - g_mid[i]) for i in idx]
            e_b = [jnp.exp(jnp.where(first[i], g_bnd[i] - g_mid[i], g_mid[i] - g_bnd[i])) for i in idx]
            qo = [jnp.where(first[i], 0.0, qd[i] * e_b[i]) for i in idx]
            ko = [jnp.where(first[i], kd[i] * e_b[i], 0.0) for i in idx]
            sc1 = [_dot_nt(qd[i], kd[i]) for i in idx]
            sc2 = [_dot_nt(qo[i], ko[i]) for i in idx]
            q_in = [qd[i] * jnp.exp(g_mid[i]) for i in idx]
            yield
            attn = []
            for i in idx:
                ii, jj, same, incl, _ = masks[items[i][2]]
                attn.append(jnp.where(incl & ((ii ^ jj) < SUB), sc1[i], 0.0) + jnp.where(same, sc2[i], 0.0))
            o_intra = [_dot(attn[i], v[i]) for i in idx]
        d_st = [[_dot(jnp.transpose(v[i][c * CHUNK:(c + 1) * CHUNK].astype(F32)), k_out[i][c * CHUNK:(c + 1) * CHUNK])
                 for c in range(cps)] for i in idx]
        yield
        st = [st_s[it[1], it[2]] for it in items]
        st_in = [[None] * cps for _ in idx]
        for step_c in range(cps):
            for i in idx:
                c = (cps - 1 - step_c) if revs[i] else step_c
                st_in[i][c] = st[i]
                st[i] = st[i] * jnp.exp(g_tot[i][c * CHUNK:c * CHUNK + 1, :]) + d_st[i][c]
        outs = None
        if mode is not None:
            outs = [[o_intra[i][c * CHUNK:(c + 1) * CHUNK] + _dot_nt(q_in[i][c * CHUNK:(c + 1) * CHUNK], st_in[i][c])
                     for c in range(cps)] for i in idx]
        for i in idx:
            _, hh, d, _ = items[i]
            st_s[hh, d] = st[i]
            if mode == "set":
                o_ref[0, pl.ds(r0[i], SUPER), hs[i]] = jnp.concatenate(outs[i], axis=0)
            elif mode == "add":
                o_ref[0, pl.ds(r0[i], SUPER), hs[i]] += jnp.concatenate(outs[i], axis=0)

    def interleave(*gens):
        live = list(gens)
        while live:
            for gen in list(live):
                try:
                    next(gen)
                except StopIteration:
                    live.remove(gen)

    lat = ((ql_ref, gfl_ref, kfl_ref, vl_ref), (ql_ref, gbl_ref, kbl_ref, vl_ref))
    ctx = ((qc_ref, gfc_ref, kfc_ref, vc_ref), (qc_ref, gbc_ref, kbc_ref, vc_ref))
    st_s[...] = jnp.zeros_like(st_s)
    interleave(scan_supers([(ctx[d], hh, d, 0) for hh in range(HP) for d in range(2)], None))

    def run(lo, hi, mode):
        def body(n, _):
            first = GLA_GROUP * n
            interleave(*[scan_supers([(lat[d], hh, d, (first + e) if d == 0 else (ns_lat - 1 - first - e))
                                      for hh in range(HP) for d in range(2)], mode) for e in range(GLA_GROUP)])
            return 0
        lax.fori_loop(lo, hi, body, 0)

    assert ns_lat % (2 * GLA_GROUP) == 0
    run(0, ns_lat // (2 * GLA_GROUP), "set")
    run(ns_lat // (2 * GLA_GROUP), ns_lat // GLA_GROUP, "add")


def _gla(lat, ctx):
    bsz, n_lat, _ = lat[0].shape
    n_ctx = ctx[0].shape[1]
    width = HP * DH
    groups = HEADS // HP

    def specs(n):
        one = lambda j0: pl.BlockSpec((1, n, width), lambda b, p: (b, 0, j0 + p))
        return [one(0), one(0), one(groups), one(0), one(groups), one(0)]

    def args(t):
        q, g, k, v = t
        return [q, g, g, k, k, v]

    return pl.pallas_call(
        functools.partial(_gla_kernel, n_ctx=n_ctx, n_lat=n_lat),
        grid=(bsz, groups),
        in_specs=specs(n_lat) + specs(n_ctx),
        out_specs=pl.BlockSpec((1, n_lat, width), lambda b, p: (b, 0, p)),
        out_shape=jax.ShapeDtypeStruct((bsz, n_lat, QK), F32),
        scratch_shapes=[pltpu.VMEM((HP, 2, DH, DH), F32)],
        compiler_params=pltpu.CompilerParams(vmem_limit_bytes=VMEM_LIMIT),
        name="gla",
    )(*args(lat), *args(ctx))


def _gated_rms(o, gain, gate):
    parts = []
    for hh in range(HEADS):
        oh = o[:, hh * DH:(hh + 1) * DH]
        ms = jnp.mean(oh * oh, axis=-1, keepdims=True)
        parts.append(oh * lax.rsqrt(ms + RMS_EPS))
    return jnp.concatenate(parts, axis=1) * gain * _silu(gate)


def _post_kernel(x_ref, oa_ref, ob_ref, shift_ref, scale_ref, gate_ref, w_ref, wa_ref, wb_ref, wo_ref,
                 ga_ref, gb_ref, lng_ref, lnb_ref, out_ref, ob_s):
    for cl in range(COLS_PER_TILE):
        ob_s[:, cl, :] = ob_ref[0, cl]

    def part_stages(part):
        rows = GRID_W // POST_PARTS
        xt = _load_raster(x_ref, True, part, POST_PARTS)
        u = (_layer_norm(xt) * (1.0 + scale_ref[0]) + shift_ref[0]).astype(BF16)
        yield
        za = jnp.dot(u, w_ref[:, 0:QK], preferred_element_type=F32)
        y_a = _dot(_gated_rms(_load_raster(oa_ref, True, part, POST_PARTS), ga_ref[...], za), wa_ref[...])
        yield
        zb = jnp.dot(u, w_ref[:, QK:2 * QK], preferred_element_type=F32)
        ob = ob_s[part * rows:(part + 1) * rows].reshape(TOK_TILE // POST_PARTS, QK)
        y_b = _dot(_gated_rms(ob, gb_ref[...], zb), wb_ref[...])
        yield
        mix = _sigmoid(jnp.dot(u, w_ref[:, 2 * QK:2 * QK + D_MODEL], preferred_element_type=F32)) * y_a
        yield
        mix = mix + _sigmoid(jnp.dot(u, w_ref[:, 2 * QK + D_MODEL:], preferred_element_type=F32)) * y_b
        yield
        sub = _dot(mix, wo_ref[...])
        hres = DEEPNORM_ALPHA * xt + gate_ref[0] * sub
        y = _layer_norm(hres) * lng_ref[...] + lnb_ref[...]
        _store_raster(out_ref, y, True, part, POST_PARTS)

    _interleave_skewed([part_stages(p) for p in range(POST_PARTS)])


def _post_project(x, oa, ob_cm, shift, scale, gate, w_post, w_a_out, w_b_out, w_out, a_gain, b_gain, ln_g, ln_b):
    bsz, length, _ = x.shape
    rows = length // GRID_W
    nj = GRID_W // COLS_PER_TILE
    const = lambda shape: pl.BlockSpec(shape, lambda b, j: tuple(0 for _ in shape))
    rast = lambda width: pl.BlockSpec((1, rows, COLS_PER_TILE, width), lambda b, j: (b, 0, j, 0))
    modv = pl.BlockSpec((1, 1, D_MODEL), lambda b, j: (b, 0, 0))
    out = pl.pallas_call(
        _post_kernel,
        grid=(bsz, nj),
        in_specs=[rast(D_MODEL), rast(QK),
                  pl.BlockSpec((1, COLS_PER_TILE, rows, QK), lambda b, j: (b, j, 0, 0)),
                  modv, modv, modv,
                  const((D_MODEL, N_POST)), const((QK, D_MODEL)), const((QK, D_MODEL)),
                  const((D_MODEL, D_MODEL)), const((1, QK)), const((1, QK)),
                  const((1, D_MODEL)), const((1, D_MODEL))],
        out_specs=rast(D_MODEL),
        out_shape=jax.ShapeDtypeStruct((bsz, rows, GRID_W, D_MODEL), F32),
        scratch_shapes=[pltpu.VMEM((rows, COLS_PER_TILE, QK), F32)],
        compiler_params=pltpu.CompilerParams(vmem_limit_bytes=VMEM_LIMIT),
        name="post",
    )(x.reshape(bsz, rows, GRID_W, D_MODEL), oa.reshape(bsz, rows, GRID_W, QK),
      ob_cm.reshape(bsz, GRID_W, rows, QK), shift, scale, gate,
      w_post, w_a_out, w_b_out, w_out, a_gain, b_gain, ln_g, ln_b)
    return out.reshape(bsz, length, D_MODEL)


def kernel(x, c, ctx, c_ctx, w_mod, b_mod, w_in, conv_w, a_log, dt_bias, lb_param, a_norm_g, b_norm_g,
           w_a_out, w_b_out, w_out, ln_g, ln_b):
    assert w_mod.shape[0] == DEPTH
    bsz, length, _ = x.shape
    n_ctx = ctx.shape[1]
    f32 = lambda t: t.astype(F32)

    w = f32(w_in[0])
    o_alpha = 3 * QK
    o_agate = o_alpha + 4 * HEADS
    o_bq = o_agate + QK
    o_bgate = o_bq + 4 * QK
    w_pre = jnp.concatenate([w[:, :o_alpha], jnp.pad(w[:, o_alpha:o_agate], ((0, 0), (0, LANES - 4 * HEADS))),
                             w[:, o_bq:o_bgate]], axis=1).astype(BF16)
    w_post = jnp.concatenate([w[:, o_agate:o_bq], w[:, o_bgate:]], axis=1).astype(BF16)
    alog_vec = jnp.pad(f32(a_log[0]).reshape(1, 2 * HEADS), ((0, 0), (0, LANES - 2 * HEADS)))
    dtb_vec = jnp.pad(f32(dt_bias[0]).reshape(1, 2 * HEADS), ((0, 0), (0, LANES - 2 * HEADS)))
    lbp = f32(lb_param).reshape(DEPTH + 1, 2 * QK)

    mod = _modulation(f32(c), f32(c_ctx), f32(w_mod[0]), f32(b_mod[0]))
    shift_l = mod[:bsz, None, 0:D_MODEL]
    scale_l = mod[:bsz, None, D_MODEL:2 * D_MODEL]
    gate_l = mod[:bsz, None, 2 * D_MODEL:]
    shift_c = mod[bsz:bsz + 1, None, 0:D_MODEL]
    scale_c = mod[bsz:bsz + 1, None, D_MODEL:2 * D_MODEL]

    lat = _pre_project(f32(x), shift_l, scale_l, w_pre, alog_vec, dtb_vec, lbp, cm=True)
    cx = _pre_project(f32(ctx).reshape(bsz * n_ctx, D_MODEL), shift_c, scale_c, w_pre, alog_vec, dtb_vec, lbp,
                      cm=False)
    aqkv_l = lat[0].reshape(bsz, length, 3 * QK)
    ab_l = lat[1].reshape(bsz, length, LANES)
    gla_l = tuple(t.reshape(bsz, length, t.shape[-1]) for t in lat[2:])
    aqkv_c = cx[0].reshape(bsz, n_ctx, 3 * QK)
    ab_c = cx[1].reshape(bsz, n_ctx, LANES)
    gla_c = tuple(t.reshape(bsz, n_ctx, t.shape[-1]) for t in cx[2:])

    oa = _gdn(aqkv_l, aqkv_c, ab_l, ab_c, f32(conv_w[0]))
    ob_cm = _gla(gla_l, gla_c)

    a_gain = jnp.tile(f32(a_norm_g[0]), HEADS).reshape(1, QK)
    b_gain = jnp.tile(f32(b_norm_g[0]), HEADS).reshape(1, QK)
    out = _post_project(f32(x), oa, ob_cm, shift_l, scale_l, gate_l, w_post,
                        f32(w_a_out[0]).astype(BF16), f32(w_b_out[0]).astype(BF16), f32(w_out[0]).astype(BF16),
                        a_gain, b_gain, f32(ln_g[0]).reshape(1, D_MODEL), f32(ln_b[0]).reshape(1, D_MODEL))
    return out.astype(x.dtype)
```
